```python
import jax, jax.numpy as jnp
from jax import lax
import numpy as np

D_MODEL = 1024
BATCH = 8
SEQ = 2048
DEPTH = 4
DEC_BATCH = 128
DEC_SEQ = 4
PAST_LEN = 16384
PAGE_SIZE = 128

N_MIXERS = 2
N_CONV = (DEPTH + 1) // 2
N_POOL = DEPTH // 2
CONV_W = 3
POOL_WINDOWS = (2, 4, 8, 16)
N_POOL_GROUPS = len(POOL_WINDOWS)
POOL_GROUP = D_MODEL // N_POOL_GROUPS
POOL_HIST = max(POOL_WINDOWS) - 1
D_FF = 2816
N_NORMS = 6
EPS = 1e-6

kernel_name = "macaron_conv_pool_hybrid_step"


def rmsnorm(x, g):
    xf = x.astype(jnp.float32)
    y = xf * lax.rsqrt(jnp.mean(xf * xf, axis=-1, keepdims=True) + EPS)
    return (y * g.astype(jnp.float32)).astype(x.dtype)


def swiglu(h, w_gate, w_up, w_down):
    a = jnp.einsum('bsd,df->bsf', h, w_gate)
    b = jnp.einsum('bsd,df->bsf', h, w_up)
    return jnp.einsum('bsf,fd->bsd', jax.nn.silu(a) * b, w_down)


def conv_mixer(u, hist, w_in, kernel, w_out):
    s = u.shape[1]
    bcv = jnp.einsum('bsd,de->bse', u, w_in)
    gate_b, gate_c, v = jnp.split(bcv, 3, axis=-1)
    z = gate_c * v
    zf = jnp.concatenate([hist.astype(z.dtype), z], axis=1)
    conv = sum(kernel[k] * zf[:, k:k + s] for k in range(CONV_W))
    y = jnp.einsum('bsd,de->bse', gate_b * conv, w_out)
    return y, zf[:, -(CONV_W - 1):]


def pool_mixer(u, hist, start_pos, w_group, scale):
    s = u.shape[1]
    full = jnp.concatenate([hist.astype(u.dtype), u], axis=1)
    ff = full.astype(jnp.float32)
    cs = jnp.concatenate([jnp.zeros_like(ff[:, :1]), jnp.cumsum(ff, axis=1)], axis=1)
    pos = start_pos + jnp.arange(s)
    uf = u.astype(jnp.float32)
    outs = []
    for g, w in enumerate(POOL_WINDOWS):
        sl = slice(g * POOL_GROUP, (g + 1) * POOL_GROUP)
        lo = POOL_HIST + 1
        win_sum = cs[:, lo:lo + s, sl] - cs[:, lo - w:lo - w + s, sl]
        count = jnp.minimum(pos + 1, w).astype(jnp.float32)[None, :, None]
        diff = win_sum / count - uf[:, :, sl]
        outs.append(jnp.einsum('bsc,cd->bsd', diff.astype(u.dtype), w_group[g]))
    y = jnp.concatenate(outs, axis=-1) * scale
    return y, full[:, -POOL_HIST:]


def trunk(x, conv_hist, pool_hist, start_pos, norm_gains, ffn_w_gate, ffn_w_up, ffn_w_down,
          conv_w_in, conv_kernel, conv_w_out, pool_w_group, pool_scale):
    new_conv, new_pool = [], []
    for i in range(DEPTH):
        g = norm_gains[i]
        h = rmsnorm(x, g[0])
        x = x + 0.5 * rmsnorm(swiglu(h, ffn_w_gate[i, 0], ffn_w_up[i, 0], ffn_w_down[i, 0]), g[1])
        h = rmsnorm(x, g[2])
        j = i // N_MIXERS
        if i % N_MIXERS == 0:
            m, nh = conv_mixer(h, conv_hist[j], conv_w_in[j], conv_kernel[j], conv_w_out[j])
            new_conv.append(nh)
        else:
            m, nh = pool_mixer(h, pool_hist[j], start_pos, pool_w_group[j], pool_scale[j])
            new_pool.append(nh)
        x = x + rmsnorm(m, g[3])
        h = rmsnorm(x, g[4])
        x = x + 0.5 * rmsnorm(swiglu(h, ffn_w_gate[i, 1], ffn_w_up[i, 1], ffn_w_down[i, 1]), g[5])
    return x, jnp.stack(new_conv), jnp.stack(new_pool)


def setup_inputs(seed: int = 0) -> dict:
    key = jax.random.key(seed)
    ks = jax.random.split(key, 14)
    f32 = jnp.float32
    nrm = lambda k, shape, sc: jax.random.normal(k, shape, f32) * sc
    return {
        "x_prompt": nrm(ks[0], (BATCH, SEQ, D_MODEL), 1.0),
        "x_sample": nrm(ks[1], (DEC_BATCH, DEC_SEQ, D_MODEL), 1.0),
        "state_conv": nrm(ks[2], (N_CONV, DEC_BATCH, CONV_W - 1, D_MODEL), 1.0),
        "state_pool": nrm(ks[3], (N_POOL, DEC_BATCH, POOL_HIST, D_MODEL), 1.0),
        "norm_gains": 1.0 + nrm(ks[4], (DEPTH, N_NORMS, D_MODEL), 0.1),
        "ffn_w_gate": nrm(ks[5], (DEPTH, 2, D_MODEL, D_FF), D_MODEL ** -0.5),
        "ffn_w_up": nrm(ks[6], (DEPTH, 2, D_MODEL, D_FF), D_MODEL ** -0.5),
        "ffn_w_down": nrm(ks[7], (DEPTH, 2, D_FF, D_MODEL), D_FF ** -0.5),
        "conv_w_in": nrm(ks[8], (N_CONV, D_MODEL, 3 * D_MODEL), D_MODEL ** -0.5),
        "conv_kernel": nrm(ks[9], (N_CONV, CONV_W, D_MODEL), CONV_W ** -0.5),
        "conv_w_out": nrm(ks[10], (N_CONV, D_MODEL, D_MODEL), D_MODEL ** -0.5),
        "pool_w_group": nrm(ks[11], (N_POOL, N_POOL_GROUPS, POOL_GROUP, POOL_GROUP), POOL_GROUP ** -0.5),
        "pool_scale": 1.0 + nrm(ks[12], (N_POOL, D_MODEL), 0.1),
    }


def reference(x_prompt, x_sample, state_conv, state_pool, norm_gains, ffn_w_gate, ffn_w_up,
              ffn_w_down, conv_w_in, conv_kernel, conv_w_out, pool_w_group, pool_scale):
    conv_hist_p = jnp.zeros((N_CONV, x_prompt.shape[0], CONV_W - 1, D_MODEL), x_prompt.dtype)
    pool_hist_p = jnp.zeros((N_POOL, x_prompt.shape[0], POOL_HIST, D_MODEL), x_prompt.dtype)
    y_prompt, new_conv_prompt, new_pool_prompt = trunk(
        x_prompt, conv_hist_p, pool_hist_p, 0, norm_gains, ffn_w_gate, ffn_w_up, ffn_w_down,
        conv_w_in, conv_kernel, conv_w_out, pool_w_group, pool_scale)
    y_sample, new_conv_sample, new_pool_sample = trunk(
        x_sample, state_conv, state_pool, PAST_LEN, norm_gains, ffn_w_gate, ffn_w_up, ffn_w_down,
        conv_w_in, conv_kernel, conv_w_out, pool_w_group, pool_scale)
    return (y_prompt, y_sample, new_conv_prompt, new_conv_sample, new_pool_prompt, new_pool_sample)
```

```python
import functools

import jax
import jax.numpy as jnp
from jax import lax
from jax.experimental import pallas as pl
from jax.experimental.pallas import tpu as pltpu

D_MODEL = 1024
D_FF = 2816
DEPTH = 4
CONV_W = 3
POOL_WINDOWS = (2, 4, 8, 16)
POOL_GROUP = D_MODEL // len(POOL_WINDOWS)
POOL_HIST = max(POOL_WINDOWS) - 1
EPS = 1e-6

SUBLANES = 8
FFN_ROWS = 512
MIX_ROWS = 512
POOL_PAD_STEPS = 16
VMEM_LIMIT_BYTES = 56 * 1024 * 1024

_F32 = jnp.float32
_BF16 = jnp.bfloat16


def _rmsnorm(x, g):
    ms = jnp.mean(x * x, axis=-1, keepdims=True)
    return x * lax.rsqrt(ms + EPS) * g


def _dot(a, b):
    return jnp.dot(a, b, preferred_element_type=_F32)


def _resident(shape):
    nd = len(shape)
    return pl.BlockSpec(shape, lambda *_: (0,) * nd, pipeline_mode=pl.Buffered(1))


def _ffn_kernel(x_ref, gpre_ref, gpost_ref, wg_ref, wu_ref, wd_ref, o_ref):
    x = x_ref[...]
    h = _rmsnorm(x, gpre_ref[...]).astype(_BF16)
    a = _dot(h, wg_ref[...])
    b = _dot(h, wu_ref[...])
    g = (a * jax.nn.sigmoid(a) * b).astype(_BF16)
    y = _dot(g, wd_ref[...])
    o_ref[...] = x + 0.5 * _rmsnorm(y, gpost_ref[...])


def _ffn_sublayer(x_all, gpre, gpost, wg, wu, wd):
    n = x_all.shape[0]
    assert n % FFN_ROWS == 0
    row_spec = pl.BlockSpec((FFN_ROWS, D_MODEL), lambda i: (i, 0))
    return pl.pallas_call(
        _ffn_kernel,
        grid=(n // FFN_ROWS,),
        in_specs=[row_spec, _resident((1, D_MODEL)), _resident((1, D_MODEL)),
                  _resident((D_MODEL, D_FF)), _resident((D_MODEL, D_FF)),
                  _resident((D_FF, D_MODEL))],
        out_specs=row_spec,
        out_shape=jax.ShapeDtypeStruct(x_all.shape, _F32),
        input_output_aliases={0: 0},
        compiler_params=pltpu.CompilerParams(
            dimension_semantics=("arbitrary",), vmem_limit_bytes=VMEM_LIMIT_BYTES),
        name="ffn_sublayer",
    )(x_all, gpre, gpost, wg, wu, wd)


def _conv_kernel(x_ref, hist_ref, gpre_ref, gpost_ref, win_ref, ck_ref, wout_ref,
                 o_ref, newhist_ref, ext_ref, *, rows, stride, pad, zero_first):
    if zero_first:
        @pl.when(pl.program_id(1) == 0)
        def _():
            ext_ref[pl.ds(0, pad), :] = jnp.zeros((pad, D_MODEL), _F32)
    else:
        ext_ref[pl.ds(0, pad), :] = hist_ref[...]

    x = x_ref[...]
    h = _rmsnorm(x, gpre_ref[...]).astype(_BF16)
    gate_c = _dot(h, win_ref[:, D_MODEL:2 * D_MODEL])
    v = _dot(h, win_ref[:, 2 * D_MODEL:])
    z = gate_c * v
    ext_ref[pl.ds(pad, rows), :] = z
    ck = ck_ref[...]
    conv = (ck[0:1] * ext_ref[pl.ds(pad - 2 * stride, rows), :]
            + ck[1:2] * ext_ref[pl.ds(pad - stride, rows), :]
            + ck[2:3] * z)
    gate_b = _dot(h, win_ref[:, :D_MODEL])
    y = _dot((gate_b * conv).astype(_BF16), wout_ref[...])
    o_ref[...] = x + _rmsnorm(y, gpost_ref[...])

    tail = ext_ref[pl.ds(rows, pad), :]
    newhist_ref[...] = tail.reshape(newhist_ref.shape)
    ext_ref[pl.ds(0, pad), :] = tail


def _pool_kernel(x_ref, hist_ref, gpre_ref, gpost_ref, wgrp_ref, scale_ref,
                 o_ref, newhist_ref, ext_ref, *, rows, stride, pad, zero_first,
                 start_pos):
    if zero_first:
        @pl.when(pl.program_id(1) == 0)
        def _():
            ext_ref[pl.ds(0, pad), :] = jnp.zeros((pad, D_MODEL), _F32)
    else:
        ext_ref[pl.ds(0, pad), :] = hist_ref[...]

    x = x_ref[...]
    u = _rmsnorm(x, gpre_ref[...])
    ext_ref[pl.ds(pad, rows), :] = u

    step = lax.broadcasted_iota(jnp.int32, (rows, 1), 0) // stride
    if zero_first:
        step = step + pl.program_id(1) * (rows // stride)
    pos = step + start_pos

    ys = []
    for g, w in enumerate(POOL_WINDOWS):
        cols = slice(g * POOL_GROUP, (g + 1) * POOL_GROUP)
        acc = ext_ref[:, cols]
        span = 1
        while span < w:
            acc = acc + pltpu.roll(acc, span * stride, axis=0)
            span *= 2
        win_sum = acc[pad:, :]
        count = jnp.minimum(pos + 1, w).astype(_F32)
        diff = win_sum / count - u[:, cols]
        ys.append(_dot(diff.astype(_BF16), wgrp_ref[g]))
    y = jnp.concatenate(ys, axis=-1) * scale_ref[...]
    o_ref[...] = x + _rmsnorm(y, gpost_ref[...])

    tail = ext_ref[pl.ds(rows, pad), :]
    newhist_ref[...] = tail.reshape(newhist_ref.shape)
    ext_ref[pl.ds(0, pad), :] = tail


def _mixer_call(kernel, name, x_all, hist, vec_args, mat_args, *, n_seq, seq_len,
                stride, pad, row_offset, zero_first, **static):
    if stride == 1:
        rows = min(MIX_ROWS, seq_len)
        grid = (n_seq, seq_len // rows)
        blocks_per_seq = seq_len // rows
        base = row_offset // rows
        row_map = lambda b, j: (base + b * blocks_per_seq + j, 0)
        newhist_shape = (n_seq, pad, D_MODEL)
        newhist_spec = pl.BlockSpec((1, pad, D_MODEL), lambda b, j: (b, 0, 0))
    else:
        rows = n_seq * seq_len
        grid = (1, 1)
        base = row_offset // rows
        row_map = lambda b, j: (base, 0)
        newhist_shape = (pad, D_MODEL)
        newhist_spec = pl.BlockSpec((pad, D_MODEL), lambda b, j: (0, 0))
    assert row_offset % rows == 0
    row_spec = pl.BlockSpec((rows, D_MODEL), row_map)
    hist_spec = _resident(hist.shape)
    body = functools.partial(kernel, rows=rows, stride=stride, pad=pad,
                             zero_first=zero_first, **static)
    out, newhist = pl.pallas_call(
        body,
        grid=grid,
        in_specs=[row_spec, hist_spec]
        + [_resident(a.shape) for a in vec_args]
        + [_resident(a.shape) for a in mat_args],
        out_specs=[row_spec, newhist_spec],
        out_shape=[jax.ShapeDtypeStruct(x_all.shape, _F32),
                   jax.ShapeDtypeStruct(newhist_shape, _F32)],
        scratch_shapes=[pltpu.VMEM((pad + rows, D_MODEL), _F32)],
        input_output_aliases={0: 0},
        compiler_params=pltpu.CompilerParams(
            dimension_semantics=("arbitrary", "arbitrary"),
            vmem_limit_bytes=VMEM_LIMIT_BYTES),
        name=name,
    )(x_all, hist, *vec_args, *mat_args)
    return out, newhist


def kernel(x_prompt, x_sample, state_conv, state_pool, norm_gains, ffn_w_gate, ffn_w_up,
           ffn_w_down, conv_w_in, conv_kernel, conv_w_out, pool_w_group, pool_scale):
    batch, seq, _ = x_prompt.shape
    dec_batch, dec_seq, _ = x_sample.shape
    n_prompt = batch * seq
    n_sample = dec_batch * dec_seq
    past_len = 16384

    x_all = jnp.concatenate(
        [x_prompt.reshape(n_prompt, D_MODEL),
         jnp.swapaxes(x_sample, 0, 1).reshape(n_sample, D_MODEL)], axis=0)

    gains = norm_gains.reshape(DEPTH, 6, 1, D_MODEL)
    wg = ffn_w_gate.astype(_BF16)
    wu = ffn_w_up.astype(_BF16)
    wd = ffn_w_down.astype(_BF16)
    w_in = conv_w_in.astype(_BF16)
    w_out = conv_w_out.astype(_BF16)
    w_grp = pool_w_group.astype(_BF16)

    conv_pad_p = SUBLANES
    conv_pad_s = (CONV_W - 1) * dec_batch
    pool_pad_p = POOL_PAD_STEPS
    pool_pad_s = POOL_PAD_STEPS * dec_batch
    dummy_hist = jnp.zeros((SUBLANES, D_MODEL), _F32)

    new_conv_p, new_conv_s, new_pool_p, new_pool_s = [], [], [], []
    for i in range(DEPTH):
        g = gains[i]
        x_all = _ffn_sublayer(x_all, g[0], g[1], wg[i, 0], wu[i, 0], wd[i, 0])
        j = i // 2
        if i % 2 == 0:
            mats = (w_in[j], conv_kernel[j], w_out[j])
            x_all, hp = _mixer_call(
                _conv_kernel, "conv_prompt", x_all, dummy_hist, (g[2], g[3]), mats,
                n_seq=batch, seq_len=seq, stride=1, pad=conv_pad_p, row_offset=0,
                zero_first=True)
            hist_s = jnp.swapaxes(state_conv[j], 0, 1).reshape(conv_pad_s, D_MODEL)
            x_all, hs = _mixer_call(
                _conv_kernel, "conv_sample", x_all, hist_s, (g[2], g[3]), mats,
                n_seq=dec_batch, seq_len=dec_seq, stride=dec_batch, pad=conv_pad_s,
                row_offset=n_prompt, zero_first=False)
            new_conv_p.append(hp[:, conv_pad_p - (CONV_W - 1):, :])
            new_conv_s.append(jnp.swapaxes(
                hs.reshape(CONV_W - 1, dec_batch, D_MODEL), 0, 1))
        else:
            mats = (w_grp[j], pool_scale[j].reshape(1, D_MODEL))
            x_all, hp = _mixer_call(
                _pool_kernel, "pool_prompt", x_all, dummy_hist, (g[2], g[3]), mats,
                n_seq=batch, seq_len=seq, stride=1, pad=pool_pad_p, row_offset=0,
                zero_first=True, start_pos=0)
            hist_s = jnp.swapaxes(state_pool[j], 0, 1)
            hist_s = jnp.concatenate(
                [jnp.zeros((POOL_PAD_STEPS - POOL_HIST, dec_batch, D_MODEL), _F32), hist_s],
                axis=0).reshape(pool_pad_s, D_MODEL)
            x_all, hs = _mixer_call(
                _pool_kernel, "pool_sample", x_all, hist_s, (g[2], g[3]), mats,
                n_seq=dec_batch, seq_len=dec_seq, stride=dec_batch, pad=pool_pad_s,
                row_offset=n_prompt, zero_first=False, start_pos=past_len)
            new_pool_p.append(hp[:, pool_pad_p - POOL_HIST:, :])
            new_pool_s.append(jnp.swapaxes(
                hs.reshape(POOL_PAD_STEPS, dec_batch, D_MODEL)[POOL_PAD_STEPS - POOL_HIST:],
                0, 1))
        x_all = _ffn_sublayer(x_all, g[4], g[5], wg[i, 1], wu[i, 1], wd[i, 1])

    y_prompt = x_all[:n_prompt].reshape(batch, seq, D_MODEL)
    y_sample = jnp.swapaxes(
        x_all[n_prompt:].reshape(dec_seq, dec_batch, D_MODEL), 0, 1)
    return (y_prompt, y_sample, jnp.stack(new_conv_p), jnp.stack(new_conv_s),
            jnp.stack(new_pool_p), jnp.stack(new_pool_s))
```

```python
import functools

import jax
import jax.numpy as jnp
from jax import lax
from jax.experimental import pallas as pl
from jax.experimental.pallas import tpu as pltpu

D_MODEL = 1024
D_FF = 2816
DEPTH = 4
N_NORMS = 6
CONV_W = 3
POOL_WINDOWS = (2, 4, 8, 16)
POOL_GROUP = D_MODEL // len(POOL_WINDOWS)
POOL_HIST = max(POOL_WINDOWS) - 1
PAST_LEN = 16384
EPS = 1e-6

SUBLANES = 8
FFN_ROWS = 512
FFN_SUB_ROWS = 256
MIX_ROWS = 512
POOL_PAD_STEPS = 16
VMEM_LIMIT_BYTES = 56 * 1024 * 1024

_F32 = jnp.float32
_BF16 = jnp.bfloat16


def _rmsnorm(x, g):
    ms = jnp.mean(x * x, axis=-1, keepdims=True)
    return x * lax.rsqrt(ms + EPS) * g


def _dot(a, b):
    return jnp.dot(a, b, preferred_element_type=_F32)


def _resident(shape):
    nd = len(shape)
    return pl.BlockSpec(shape, lambda *_: (0,) * nd, pipeline_mode=pl.Buffered(1))


def _stacked(arr, idx):
    tail = arr.shape[1:]
    return pl.BlockSpec((None,) + tail, lambda *_: (idx,) + (0,) * len(tail),
                        pipeline_mode=pl.Buffered(1))


def _ffn_kernel(*refs, mode, n_prompt_blocks):
    if mode == "first":
        xp_ref, xs_ref, gpre_ref, gpost_ref, wg_ref, wu_ref, wd_ref, o_ref = refs
    elif mode == "last":
        x_ref, gpre_ref, gpost_ref, wg_ref, wu_ref, wd_ref, yp_ref, ys_ref = refs
    else:
        x_ref, gpre_ref, gpost_ref, wg_ref, wu_ref, wd_ref, o_ref = refs
    is_prompt = pl.program_id(0) < n_prompt_blocks

    for s in range(FFN_ROWS // FFN_SUB_ROWS):
        rows = pl.ds(s * FFN_SUB_ROWS, FFN_SUB_ROWS)
        if mode == "first":
            x = jnp.where(is_prompt, xp_ref[rows, :], xs_ref[rows, :])
        else:
            x = x_ref[rows, :]
        h = _rmsnorm(x, gpre_ref[...]).astype(_BF16)
        a = _dot(h, wg_ref[...])
        b = _dot(h, wu_ref[...])
        g = (a * jax.nn.sigmoid(a) * b).astype(_BF16)
        y = _dot(g, wd_ref[...])
        out = x + 0.5 * _rmsnorm(y, gpost_ref[...])
        if mode == "last":
            @pl.when(is_prompt)
            def _():
                yp_ref[rows, :] = out

            @pl.when(jnp.logical_not(is_prompt))
            def _():
                ys_ref[rows, :] = out
        else:
            o_ref[rows, :] = out


def _ffn_sublayer(xs, gains, g_idx, weights, w_idx, *, mode, n_prompt, n_sample):
    n = n_prompt + n_sample
    assert n_prompt % FFN_ROWS == 0 and n_sample == FFN_ROWS
    npb = n_prompt // FFN_ROWS
    row_spec = pl.BlockSpec((FFN_ROWS, D_MODEL), lambda i: (i, 0))
    prompt_spec = pl.BlockSpec((FFN_ROWS, D_MODEL), lambda i: (jnp.minimum(i, npb - 1), 0))
    sample_spec = pl.BlockSpec((FFN_ROWS, D_MODEL), lambda i: (0, 0))
    x_all_shape = jax.ShapeDtypeStruct((n, D_MODEL), _F32)
    if mode == "first":
        x_specs, out_specs, out_shape, aliases = [prompt_spec, sample_spec], row_spec, x_all_shape, {}
    elif mode == "last":
        x_specs, aliases = [row_spec], {}
        out_specs = [prompt_spec, sample_spec]
        out_shape = [jax.ShapeDtypeStruct((n_prompt, D_MODEL), _F32),
                     jax.ShapeDtypeStruct((n_sample, D_MODEL), _F32)]
    else:
        x_specs, out_specs, out_shape, aliases = [row_spec], row_spec, x_all_shape, {0: 0}
    wg, wu, wd = weights
    return pl.pallas_call(
        functools.partial(_ffn_kernel, mode=mode, n_prompt_blocks=npb),
        grid=(n // FFN_ROWS,),
        in_specs=x_specs + [_stacked(gains, g_idx), _stacked(gains, g_idx + 1),
                            _stacked(wg, w_idx), _stacked(wu, w_idx), _stacked(wd, w_idx)],
        out_specs=out_specs,
        out_shape=out_shape,
        input_output_aliases=aliases,
        compiler_params=pltpu.CompilerParams(
            dimension_semantics=("arbitrary",), vmem_limit_bytes=VMEM_LIMIT_BYTES),
        name="ffn_" + mode,
    )(*xs, gains, gains, wg, wu, wd)


def _conv_kernel(x_ref, hist_ref, gpre_ref, gpost_ref, win_ref, ck_ref, wout_ref,
                 o_ref, newhist_ref, ext_ref, *, rows, stride, pad, zero_first):
    if zero_first:
        @pl.when(pl.program_id(1) == 0)
        def _():
            ext_ref[pl.ds(0, pad), :] = jnp.zeros((pad, D_MODEL), _F32)
    else:
        ext_ref[pl.ds(0, pad), :] = hist_ref[...]

    x = x_ref[...]
    h = _rmsnorm(x, gpre_ref[...]).astype(_BF16)
    gate_c = _dot(h, win_ref[:, D_MODEL:2 * D_MODEL])
    v = _dot(h, win_ref[:, 2 * D_MODEL:])
    z = gate_c * v
    ext_ref[pl.ds(pad, rows), :] = z
    ck = ck_ref[...]
    conv = (ck[0:1] * ext_ref[pl.ds(pad - 2 * stride, rows), :]
            + ck[1:2] * ext_ref[pl.ds(pad - stride, rows), :]
            + ck[2:3] * z)
    gate_b = _dot(h, win_ref[:, :D_MODEL])
    y = _dot((gate_b * conv).astype(_BF16), wout_ref[...])
    o_ref[...] = x + _rmsnorm(y, gpost_ref[...])

    tail = ext_ref[pl.ds(rows, pad), :]
    newhist_ref[...] = tail.reshape(newhist_ref.shape)
    ext_ref[pl.ds(0, pad), :] = tail


def _pool_kernel(x_ref, hist_ref, gpre_ref, gpost_ref, wgrp_ref, scale_ref,
                 o_ref, newhist_ref, ext_ref, *, rows, stride, pad, zero_first,
                 start_pos):
    if zero_first:
        @pl.when(pl.program_id(1) == 0)
        def _():
            ext_ref[pl.ds(0, pad), :] = jnp.zeros((pad, D_MODEL), _F32)
    else:
        ext_ref[pl.ds(0, pad), :] = hist_ref[...]

    x = x_ref[...]
    u = _rmsnorm(x, gpre_ref[...])
    ext_ref[pl.ds(pad, rows), :] = u

    step = lax.broadcasted_iota(jnp.int32, (rows, 1), 0) // stride
    if zero_first:
        step = step + pl.program_id(1) * (rows // stride)
    pos = step + start_pos

    ys = []
    for g, w in enumerate(POOL_WINDOWS):
        cols = slice(g * POOL_GROUP, (g + 1) * POOL_GROUP)
        acc = ext_ref[:, cols]
        span = 1
        while span < w:
            acc = acc + pltpu.roll(acc, span * stride, axis=0)
            span *= 2
        win_sum = acc[pad:, :]
        count = jnp.minimum(pos + 1, w).astype(_F32)
        diff = win_sum / count - u[:, cols]
        ys.append(_dot(diff.astype(_BF16), wgrp_ref[g]))
    y = jnp.concatenate(ys, axis=-1) * scale_ref[...]
    o_ref[...] = x + _rmsnorm(y, gpost_ref[...])

    tail = ext_ref[pl.ds(rows, pad), :]
    newhist_ref[...] = tail.reshape(newhist_ref.shape)
    ext_ref[pl.ds(0, pad), :] = tail


def _mixer_call(kernel, name, x_all, hist, params, *, n_seq, seq_len,
                stride, pad, row_offset, zero_first, **static):
    if stride == 1:
        rows = min(MIX_ROWS, seq_len)
        blocks_per_seq = seq_len // rows
        grid = (n_seq, blocks_per_seq)
        base = row_offset // rows
        row_map = lambda b, j: (base + b * blocks_per_seq + j, 0)
        newhist_shape = (n_seq, pad, D_MODEL)
        newhist_spec = pl.BlockSpec((1, pad, D_MODEL), lambda b, j: (b, 0, 0))
    else:
        rows = n_seq * seq_len
        grid = (1, 1)
        base = row_offset // rows
        row_map = lambda b, j: (base, 0)
        newhist_shape = (pad, D_MODEL)
        newhist_spec = pl.BlockSpec((pad, D_MODEL), lambda b, j: (0, 0))
    assert row_offset % rows == 0
    row_spec = pl.BlockSpec((rows, D_MODEL), row_map)
    body = functools.partial(kernel, rows=rows, stride=stride, pad=pad,
                             zero_first=zero_first, **static)
    return pl.pallas_call(
        body,
        grid=grid,
        in_specs=[row_spec, _resident(hist.shape)] + [_stacked(a, k) for a, k in params],
        out_specs=[row_spec, newhist_spec],
        out_shape=[jax.ShapeDtypeStruct(x_all.shape, _F32),
                   jax.ShapeDtypeStruct(newhist_shape, _F32)],
        scratch_shapes=[pltpu.VMEM((pad + rows, D_MODEL), _F32)],
        input_output_aliases={0: 0},
        compiler_params=pltpu.CompilerParams(
            dimension_semantics=("arbitrary", "arbitrary"),
            vmem_limit_bytes=VMEM_LIMIT_BYTES),
        name=name,
    )(x_all, hist, *[a for a, _ in params])


def kernel(x_prompt, x_sample, state_conv, state_pool, norm_gains, ffn_w_gate, ffn_w_up,
           ffn_w_down, conv_w_in, conv_kernel, conv_w_out, pool_w_group, pool_scale):
    batch, seq, _ = x_prompt.shape
    dec_batch, dec_seq, _ = x_sample.shape
    n_prompt = batch * seq
    n_sample = dec_batch * dec_seq
    n_conv = conv_w_in.shape[0]
    n_pool = pool_w_group.shape[0]

    xp_rows = x_prompt.reshape(n_prompt, D_MODEL)
    xs_rows = jnp.swapaxes(x_sample, 0, 1).reshape(n_sample, D_MODEL)

    gains = norm_gains.reshape(DEPTH * N_NORMS, 1, D_MODEL)
    ffn_w = (ffn_w_gate.astype(_BF16).reshape(2 * DEPTH, D_MODEL, D_FF),
             ffn_w_up.astype(_BF16).reshape(2 * DEPTH, D_MODEL, D_FF),
             ffn_w_down.astype(_BF16).reshape(2 * DEPTH, D_FF, D_MODEL))
    w_in = conv_w_in.astype(_BF16)
    w_out = conv_w_out.astype(_BF16)
    w_grp = pool_w_group.astype(_BF16)
    scale = pool_scale.reshape(n_pool, 1, D_MODEL)

    conv_pad_p = SUBLANES
    conv_pad_s = (CONV_W - 1) * dec_batch
    pool_pad_p = POOL_PAD_STEPS
    pool_pad_s = POOL_PAD_STEPS * dec_batch
    dummy_hist = jnp.zeros((SUBLANES, D_MODEL), _F32)
    conv_hist_s = jnp.swapaxes(state_conv, 1, 2).reshape(n_conv, conv_pad_s, D_MODEL)
    pool_hist_s = jnp.pad(jnp.swapaxes(state_pool, 1, 2),
                          ((0, 0), (POOL_PAD_STEPS - POOL_HIST, 0), (0, 0), (0, 0)))
    pool_hist_s = pool_hist_s.reshape(n_pool, pool_pad_s, D_MODEL)
    ffn = functools.partial(_ffn_sublayer, n_prompt=n_prompt, n_sample=n_sample)

    new_conv_p, new_conv_s, new_pool_p, new_pool_s = [], [], [], []
    for i in range(DEPTH):
        g0 = i * N_NORMS
        if i == 0:
            x_all = ffn((xp_rows, xs_rows), gains, g0, ffn_w, 2 * i, mode="first")
        else:
            x_all = ffn((x_all,), gains, g0, ffn_w, 2 * i, mode="mid")
        j = i // 2
        norms = [(gains, g0 + 2), (gains, g0 + 3)]
        if i % 2 == 0:
            params = norms + [(w_in, j), (conv_kernel, j), (w_out, j)]
            x_all, hp = _mixer_call(
                _conv_kernel, "conv_prompt", x_all, dummy_hist, params,
                n_seq=batch, seq_len=seq, stride=1, pad=conv_pad_p, row_offset=0,
                zero_first=True)
            x_all, hs = _mixer_call(
                _conv_kernel, "conv_sample", x_all, conv_hist_s[j], params,
                n_seq=dec_batch, seq_len=dec_seq, stride=dec_batch, pad=conv_pad_s,
                row_offset=n_prompt, zero_first=False)
            new_conv_p.append(hp[:, conv_pad_p - (CONV_W - 1):, :])
            new_conv_s.append(jnp.swapaxes(
                hs.reshape(CONV_W - 1, dec_batch, D_MODEL), 0, 1))
        else:
            params = norms + [(w_grp, j), (scale, j)]
            x_all, hp = _mixer_call(
                _pool_kernel, "pool_prompt", x_all, dummy_hist, params,
                n_seq=batch, seq_len=seq, stride=1, pad=pool_pad_p, row_offset=0,
                zero_first=True, start_pos=0)
            x_all, hs = _mixer_call(
                _pool_kernel, "pool_sample", x_all, pool_hist_s[j], params,
                n_seq=dec_batch, seq_len=dec_seq, stride=dec_batch, pad=pool_pad_s,
                row_offset=n_prompt, zero_first=False, start_pos=PAST_LEN)
            new_pool_p.append(hp[:, pool_pad_p - POOL_HIST:, :])
            new_pool_s.append(jnp.swapaxes(
                hs.reshape(POOL_PAD_STEPS, dec_batch, D_MODEL)[POOL_PAD_STEPS - POOL_HIST:],
                0, 1))
        if i == DEPTH - 1:
            yp_rows, ys_rows = ffn((x_all,), gains, g0 + 4, ffn_w, 2 * i + 1, mode="last")
        else:
            x_all = ffn((x_all,), gains, g0 + 4, ffn_w, 2 * i + 1, mode="mid")

    y_prompt = yp_rows.reshape(batch, seq, D_MODEL)
    y_sample = jnp.swapaxes(ys_rows.reshape(dec_seq, dec_batch, D_MODEL), 0, 1)
    return (y_prompt, y_sample, jnp.stack(new_conv_p), jnp.stack(new_conv_s),
            jnp.stack(new_pool_p), jnp.stack(new_pool_s))
```

```python
import functools

import jax
import jax.numpy as jnp
from jax import lax
from jax.experimental import pallas as pl
from jax.experimental.pallas import tpu as pltpu

D_MODEL = 1024
D_FF = 2816
DEPTH = 4
N_NORMS = 6
CONV_W = 3
POOL_WINDOWS = (2, 4, 8, 16)
POOL_GROUP = D_MODEL // len(POOL_WINDOWS)
POOL_HIST = max(POOL_WINDOWS) - 1
PAST_LEN = 16384
EPS = 1e-6

SUBLANES = 8
FFN_ROWS = 512
FFN_SUB_ROWS = 256
MIX_ROWS = 512
POOL_PAD_STEPS = 16
VMEM_LIMIT_BYTES = 56 * 1024 * 1024

_F32 = jnp.float32
_BF16 = jnp.bfloat16


def _rmsnorm(x, g):
    ms = jnp.mean(x * x, axis=-1, keepdims=True)
    return x * lax.rsqrt(ms + EPS) * g


def _dot(a, b):
    return jnp.dot(a, b, preferred_element_type=_F32)


def _resident(shape):
    nd = len(shape)
    return pl.BlockSpec(shape, lambda *_: (0,) * nd, pipeline_mode=pl.Buffered(1))


def _stacked(arr, idx):
    tail = arr.shape[1:]
    return pl.BlockSpec((None,) + tail, lambda *_: (idx,) + (0,) * len(tail),
                        pipeline_mode=pl.Buffered(1))


def _ffn_kernel(*refs, mode, n_prompt_blocks):
    if mode == "first":
        xp_ref, xs_ref, gpre_ref, gpost_ref, wg_ref, wu_ref, wd_ref, o_ref = refs
    elif mode == "last":
        x_ref, gpre_ref, gpost_ref, wg_ref, wu_ref, wd_ref, yp_ref, ys_ref = refs
    else:
        x_ref, gpre_ref, gpost_ref, wg_ref, wu_ref, wd_ref, o_ref = refs
    is_prompt = pl.program_id(0) < n_prompt_blocks

    for s in range(FFN_ROWS // FFN_SUB_ROWS):
        rows = pl.ds(s * FFN_SUB_ROWS, FFN_SUB_ROWS)
        if mode == "first":
            x = jnp.where(is_prompt, xp_ref[rows, :], xs_ref[rows, :])
        else:
            x = x_ref[rows, :]
        h = _rmsnorm(x, gpre_ref[...])
        a = _dot(h, wg_ref[...])
        b = _dot(h, wu_ref[...])
        g = a * jax.nn.sigmoid(a) * b
        y = _dot(g, wd_ref[...])
        out = x + 0.5 * _rmsnorm(y, gpost_ref[...])
        if mode == "last":
            ys_ref[rows, :] = out
        else:
            o_ref[rows, :] = out

    if mode == "last":
        @pl.when(is_prompt)
        def _():
            yp_ref[...] = ys_ref[...]


def _ffn_sublayer(xs, gains, g_idx, weights, w_idx, *, mode, n_prompt, n_sample):
    n = n_prompt + n_sample
    assert n_prompt % FFN_ROWS == 0 and n_sample == FFN_ROWS
    npb = n_prompt // FFN_ROWS
    row_spec = pl.BlockSpec((FFN_ROWS, D_MODEL), lambda i: (i, 0))
    prompt_spec = pl.BlockSpec((FFN_ROWS, D_MODEL), lambda i: (jnp.minimum(i, npb - 1), 0))
    sample_spec = pl.BlockSpec((FFN_ROWS, D_MODEL), lambda i: (0, 0))
    x_all_shape = jax.ShapeDtypeStruct((n, D_MODEL), _F32)
    if mode == "first":
        x_specs, out_specs, out_shape, aliases = [prompt_spec, sample_spec], row_spec, x_all_shape, {}
    elif mode == "last":
        x_specs, aliases = [row_spec], {}
        out_specs = [prompt_spec, sample_spec]
        out_shape = [jax.ShapeDtypeStruct((n_prompt, D_MODEL), _F32),
                     jax.ShapeDtypeStruct((n_sample, D_MODEL), _F32)]
    else:
        x_specs, out_specs, out_shape, aliases = [row_spec], row_spec, x_all_shape, {0: 0}
    wg, wu, wd = weights
    return pl.pallas_call(
        functools.partial(_ffn_kernel, mode=mode, n_prompt_blocks=npb),
        grid=(n // FFN_ROWS,),
        in_specs=x_specs + [_stacked(gains, g_idx), _stacked(gains, g_idx + 1),
                            _stacked(wg, w_idx), _stacked(wu, w_idx), _stacked(wd, w_idx)],
        out_specs=out_specs,
        out_shape=out_shape,
        input_output_aliases=aliases,
        compiler_params=pltpu.CompilerParams(
            dimension_semantics=("arbitrary",), vmem_limit_bytes=VMEM_LIMIT_BYTES),
        name="ffn_" + mode,
    )(*xs, gains, gains, wg, wu, wd)


def _conv_kernel(x_ref, hist_ref, gpre_ref, gpost_ref, win_ref, ck_ref, wout_ref,
                 o_ref, newhist_ref, ext_ref, *, rows, stride, pad, zero_first):
    if zero_first:
        @pl.when(pl.program_id(1) == 0)
        def _():
            ext_ref[pl.ds(0, pad), :] = jnp.zeros((pad, D_MODEL), _F32)
    else:
        ext_ref[pl.ds(0, pad), :] = hist_ref[...]

    x = x_ref[...]
    h = _rmsnorm(x, gpre_ref[...]).astype(_BF16)
    gate_c = _dot(h, win_ref[:, D_MODEL:2 * D_MODEL])
    v = _dot(h, win_ref[:, 2 * D_MODEL:])
    z = gate_c * v
    ext_ref[pl.ds(pad, rows), :] = z
    ck = ck_ref[...]
    conv = (ck[0:1] * ext_ref[pl.ds(pad - 2 * stride, rows), :]
            + ck[1:2] * ext_ref[pl.ds(pad - stride, rows), :]
            + ck[2:3] * z)
    gate_b = _dot(h, win_ref[:, :D_MODEL])
    y = _dot((gate_b * conv).astype(_BF16), wout_ref[...])
    o_ref[...] = x + _rmsnorm(y, gpost_ref[...])

    tail = ext_ref[pl.ds(rows, pad), :]
    newhist_ref[...] = tail.reshape(newhist_ref.shape)
    ext_ref[pl.ds(0, pad), :] = tail


def _pool_kernel(x_ref, hist_ref, gpre_ref, gpost_ref, wgrp_ref, scale_ref,
                 o_ref, newhist_ref, ext_ref, *, rows, stride, pad, zero_first,
                 start_pos):
    if zero_first:
        @pl.when(pl.program_id(1) == 0)
        def _():
            ext_ref[pl.ds(0, pad), :] = jnp.zeros((pad, D_MODEL), _F32)
    else:
        ext_ref[pl.ds(0, pad), :] = hist_ref[...]

    x = x_ref[...]
    u = _rmsnorm(x, gpre_ref[...])
    ext_ref[pl.ds(pad, rows), :] = u

    step = lax.broadcasted_iota(jnp.int32, (rows, 1), 0) // stride
    if zero_first:
        step = step + pl.program_id(1) * (rows // stride)
    pos = step + start_pos

    ys = []
    for g, w in enumerate(POOL_WINDOWS):
        cols = slice(g * POOL_GROUP, (g + 1) * POOL_GROUP)
        acc = ext_ref[:, cols]
        span = 1
        while span < w:
            acc = acc + pltpu.roll(acc, span * stride, axis=0)
            span *= 2
        win_sum = acc[pad:, :]
        count = jnp.minimum(pos + 1, w).astype(_F32)
        diff = win_sum / count - u[:, cols]
        ys.append(_dot(diff.astype(_BF16), wgrp_ref[g]))
    y = jnp.concatenate(ys, axis=-1) * scale_ref[...]
    o_ref[...] = x + _rmsnorm(y, gpost_ref[...])

    tail = ext_ref[pl.ds(rows, pad), :]
    newhist_ref[...] = tail.reshape(newhist_ref.shape)
    ext_ref[pl.ds(0, pad), :] = tail


def _mixer_call(kernel, name, x_all, hist, params, *, n_seq, seq_len,
                stride, pad, row_offset, zero_first, **static):
    if stride == 1:
        rows = min(MIX_ROWS, seq_len)
        blocks_per_seq = seq_len // rows
        grid = (n_seq, blocks_per_seq)
        base = row_offset // rows
        row_map = lambda b, j: (base + b * blocks_per_seq + j, 0)
        newhist_shape = (n_seq, pad, D_MODEL)
        newhist_spec = pl.BlockSpec((1, pad, D_MODEL), lambda b, j: (b, 0, 0))
    else:
        rows = n_seq * seq_len
        grid = (1, 1)
        base = row_offset // rows
        row_map = lambda b, j: (base, 0)
        newhist_shape = (pad, D_MODEL)
        newhist_spec = pl.BlockSpec((pad, D_MODEL), lambda b, j: (0, 0))
    assert row_offset % rows == 0
    row_spec = pl.BlockSpec((rows, D_MODEL), row_map)
    body = functools.partial(kernel, rows=rows, stride=stride, pad=pad,
                             zero_first=zero_first, **static)
    return pl.pallas_call(
        body,
        grid=grid,
        in_specs=[row_spec, _resident(hist.shape)] + [_stacked(a, k) for a, k in params],
        out_specs=[row_spec, newhist_spec],
        out_shape=[jax.ShapeDtypeStruct(x_all.shape, _F32),
                   jax.ShapeDtypeStruct(newhist_shape, _F32)],
        scratch_shapes=[pltpu.VMEM((pad + rows, D_MODEL), _F32)],
        input_output_aliases={0: 0},
        compiler_params=pltpu.CompilerParams(
            dimension_semantics=("arbitrary", "arbitrary"),
            vmem_limit_bytes=VMEM_LIMIT_BYTES),
        name=name,
    )(x_all, hist, *[a for a, _ in params])


def kernel(x_prompt, x_sample, state_conv, state_pool, norm_gains, ffn_w_gate, ffn_w_up,
           ffn_w_down, conv_w_in, conv_kernel, conv_w_out, pool_w_group, pool_scale):
    batch, seq, _ = x_prompt.shape
    dec_batch, dec_seq, _ = x_sample.shape
    n_prompt = batch * seq
    n_sample = dec_batch * dec_seq
    n_conv = conv_w_in.shape[0]
    n_pool = pool_w_group.shape[0]

    xp_rows = x_prompt.reshape(n_prompt, D_MODEL)
    xs_rows = jnp.swapaxes(x_sample, 0, 1).reshape(n_sample, D_MODEL)

    gains = norm_gains.reshape(DEPTH * N_NORMS, 1, D_MODEL)
    ffn_w = (ffn_w_gate.reshape(2 * DEPTH, D_MODEL, D_FF),
             ffn_w_up.reshape(2 * DEPTH, D_MODEL, D_FF),
             ffn_w_down.reshape(2 * DEPTH, D_FF, D_MODEL))
    w_in = conv_w_in.astype(_BF16)
    w_out = conv_w_out.astype(_BF16)
    w_grp = pool_w_group.astype(_BF16)
    scale = pool_scale.reshape(n_pool, 1, D_MODEL)

    conv_pad_p = SUBLANES
    conv_pad_s = (CONV_W - 1) * dec_batch
    pool_pad_p = POOL_PAD_STEPS
    pool_pad_s = POOL_PAD_STEPS * dec_batch
    dummy_hist = jnp.zeros((SUBLANES, D_MODEL), _F32)
    conv_hist_s = jnp.swapaxes(state_conv, 1, 2).reshape(n_conv, conv_pad_s, D_MODEL)
    pool_hist_s = jnp.pad(jnp.swapaxes(state_pool, 1, 2),
                          ((0, 0), (POOL_PAD_STEPS - POOL_HIST, 0), (0, 0), (0, 0)))
    pool_hist_s = pool_hist_s.reshape(n_pool, pool_pad_s, D_MODEL)
    ffn = functools.partial(_ffn_sublayer, n_prompt=n_prompt, n_sample=n_sample)

    new_conv_p, new_conv_s, new_pool_p, new_pool_s = [], [], [], []
    for i in range(DEPTH):
        g0 = i * N_NORMS
        if i == 0:
            x_all = ffn((xp_rows, xs_rows), gains, g0, ffn_w, 2 * i, mode="first")
        else:
            x_all = ffn((x_all,), gains, g0, ffn_w, 2 * i, mode="mid")
        j = i // 2
        norms = [(gains, g0 + 2), (gains, g0 + 3)]
        if i % 2 == 0:
            params = norms + [(w_in, j), (conv_kernel, j), (w_out, j)]
            x_all, hp = _mixer_call(
                _conv_kernel, "conv_prompt", x_all, dummy_hist, params,
                n_seq=batch, seq_len=seq, stride=1, pad=conv_pad_p, row_offset=0,
                zero_first=True)
            x_all, hs = _mixer_call(
                _conv_kernel, "conv_sample", x_all, conv_hist_s[j], params,
                n_seq=dec_batch, seq_len=dec_seq, stride=dec_batch, pad=conv_pad_s,
                row_offset=n_prompt, zero_first=False)
            new_conv_p.append(hp[:, conv_pad_p - (CONV_W - 1):, :])
            new_conv_s.append(jnp.swapaxes(
                hs.reshape(CONV_W - 1, dec_batch, D_MODEL), 0, 1))
        else:
            params = norms + [(w_grp, j), (scale, j)]
            x_all, hp = _mixer_call(
                _pool_kernel, "pool_prompt", x_all, dummy_hist, params,
                n_seq=batch, seq_len=seq, stride=1, pad=pool_pad_p, row_offset=0,
                zero_first=True, start_pos=0)
            x_all, hs = _mixer_call(
                _pool_kernel, "pool_sample", x_all, pool_hist_s[j], params,
                n_seq=dec_batch, seq_len=dec_seq, stride=dec_batch, pad=pool_pad_s,
                row_offset=n_prompt, zero_first=False, start_pos=PAST_LEN)
            new_pool_p.append(hp[:, pool_pad_p - POOL_HIST:, :])
            new_pool_s.append(jnp.swapaxes(
                hs.reshape(POOL_PAD_STEPS, dec_batch, D_MODEL)[POOL_PAD_STEPS - POOL_HIST:],
                0, 1))
        if i == DEPTH - 1:
            yp_rows, ys_rows = ffn((x_all,), gains, g0 + 4, ffn_w, 2 * i + 1, mode="last")
        else:
            x_all = ffn((x_all,), gains, g0 + 4, ffn_w, 2 * i + 1, mode="mid")

    y_prompt = yp_rows.reshape(batch, seq, D_MODEL)
    y_sample = jnp.swapaxes(ys_rows.reshape(dec_seq, dec_batch, D_MODEL), 0, 1)
    return (y_prompt, y_sample, jnp.stack(new_conv_p), jnp.stack(new_conv_s),
            jnp.stack(new_pool_p), jnp.stack(new_pool_s))
```

```python
import functools

import jax
import jax.numpy as jnp
from jax import lax
from jax.experimental import pallas as pl
from jax.experimental.pallas import tpu as pltpu

D_MODEL = 1024
D_FF = 2816
DEPTH = 4
N_NORMS = 6
CONV_W = 3
POOL_WINDOWS = (2, 4, 8, 16)
POOL_GROUP = D_MODEL // len(POOL_WINDOWS)
POOL_HIST = max(POOL_WINDOWS) - 1
PAST_LEN = 16384
EPS = 1e-6

SUBLANES = 8
FFN_ROWS = 512
MIX_ROWS = 512
POOL_PAD_STEPS = 16
VMEM_LIMIT_BYTES = 56 * 1024 * 1024

_F32 = jnp.float32
_BF16 = jnp.bfloat16


def _rmsnorm(x, g):
    ms = jnp.mean(x * x, axis=-1, keepdims=True)
    return x * lax.rsqrt(ms + EPS) * g


def _dot(a, b):
    return jnp.dot(a, b, preferred_element_type=_F32)


def _resident(shape):
    nd = len(shape)
    return pl.BlockSpec(shape, lambda *_: (0,) * nd, pipeline_mode=pl.Buffered(1))


def _stacked(arr, idx):
    tail = arr.shape[1:]
    return pl.BlockSpec((None,) + tail, lambda *_: (idx,) + (0,) * len(tail),
                        pipeline_mode=pl.Buffered(1))


def _ffn_kernel(*refs, mode, n_prompt_blocks):
    if mode == "first":
        xp_ref, xs_ref, gpre_ref, gpost_ref, wg_ref, wu_ref, wd_ref, o_ref = refs
    elif mode == "last":
        x_ref, gpre_ref, gpost_ref, wg_ref, wu_ref, wd_ref, yp_ref, ys_ref = refs
    else:
        x_ref, gpre_ref, gpost_ref, wg_ref, wu_ref, wd_ref, o_ref = refs
    is_prompt = pl.program_id(0) < n_prompt_blocks

    if mode == "first":
        x = jnp.where(is_prompt, xp_ref[...], xs_ref[...])
    else:
        x = x_ref[...]
    h = _rmsnorm(x, gpre_ref[...])
    a = _dot(h, wg_ref[...])
    b = _dot(h, wu_ref[...])
    g = a * jax.nn.sigmoid(a) * b
    y = _dot(g, wd_ref[...])
    out = x + _rmsnorm(y, 0.5 * gpost_ref[...])
    if mode == "last":
        ys_ref[...] = out

        @pl.when(is_prompt)
        def _():
            yp_ref[...] = ys_ref[...]
    else:
        o_ref[...] = out


def _ffn_sublayer(xs, gains, g_idx, weights, w_idx, *, mode, n_prompt, n_sample):
    n = n_prompt + n_sample
    rows = FFN_ROWS
    assert n_prompt % rows == 0 and n_sample == rows
    npb = n_prompt // rows
    row_spec = pl.BlockSpec((rows, D_MODEL), lambda i: (i, 0))
    prompt_spec = pl.BlockSpec((rows, D_MODEL), lambda i: (jnp.minimum(i, npb - 1), 0))
    sample_spec = pl.BlockSpec((rows, D_MODEL), lambda i: (0, 0))
    x_all_shape = jax.ShapeDtypeStruct((n, D_MODEL), _F32)
    if mode == "first":
        x_specs, out_specs, out_shape, aliases = [prompt_spec, sample_spec], row_spec, x_all_shape, {}
    elif mode == "last":
        x_specs, aliases = [row_spec], {}
        out_specs = [prompt_spec, sample_spec]
        out_shape = [jax.ShapeDtypeStruct((n_prompt, D_MODEL), _F32),
                     jax.ShapeDtypeStruct((n_sample, D_MODEL), _F32)]
    else:
        x_specs, out_specs, out_shape, aliases = [row_spec], row_spec, x_all_shape, {0: 0}
    wg, wu, wd = weights
    return pl.pallas_call(
        functools.partial(_ffn_kernel, mode=mode, n_prompt_blocks=npb),
        grid=(n // rows,),
        in_specs=x_specs + [_stacked(gains, g_idx), _stacked(gains, g_idx + 1),
                            _stacked(wg, w_idx), _stacked(wu, w_idx), _stacked(wd, w_idx)],
        out_specs=out_specs,
        out_shape=out_shape,
        input_output_aliases=aliases,
        compiler_params=pltpu.CompilerParams(
            dimension_semantics=("arbitrary",), vmem_limit_bytes=VMEM_LIMIT_BYTES),
        name="ffn_" + mode,
    )(*xs, gains, gains, wg, wu, wd)


def _conv_kernel(x_ref, hist_ref, gpre_ref, gpost_ref, win_ref, ck_ref, wout_ref,
                 o_ref, newhist_ref, ext_ref, *, rows, stride, pad, zero_first):
    if zero_first:
        @pl.when(pl.program_id(1) == 0)
        def _():
            ext_ref[pl.ds(0, pad), :] = jnp.zeros((pad, D_MODEL), _F32)
    else:
        ext_ref[pl.ds(0, pad), :] = hist_ref[...]

    x = x_ref[...]
    h = _rmsnorm(x, gpre_ref[...]).astype(_BF16)
    gate_c = _dot(h, win_ref[:, D_MODEL:2 * D_MODEL])
    v = _dot(h, win_ref[:, 2 * D_MODEL:])
    z = gate_c * v
    ext_ref[pl.ds(pad, rows), :] = z
    ck = ck_ref[...]
    conv = (ck[0:1] * ext_ref[pl.ds(pad - 2 * stride, rows), :]
            + ck[1:2] * ext_ref[pl.ds(pad - stride, rows), :]
            + ck[2:3] * z)
    gate_b = _dot(h, win_ref[:, :D_MODEL])
    y = _dot((gate_b * conv).astype(_BF16), wout_ref[...])
    o_ref[...] = x + _rmsnorm(y, gpost_ref[...])

    tail = ext_ref[pl.ds(rows, pad), :]
    newhist_ref[...] = tail.reshape(newhist_ref.shape)
    ext_ref[pl.ds(0, pad), :] = tail


def _pool_kernel(x_ref, hist_ref, gpre_ref, gpost_ref, wgrp_ref, scale_ref,
                 o_ref, newhist_ref, ext_ref, *, rows, stride, pad, zero_first,
                 start_pos):
    if zero_first:
        @pl.when(pl.program_id(1) == 0)
        def _():
            ext_ref[pl.ds(0, pad), :] = jnp.zeros((pad, D_MODEL), _F32)
    else:
        ext_ref[pl.ds(0, pad), :] = hist_ref[...]

    x = x_ref[...]
    u = _rmsnorm(x, gpre_ref[...])
    ext_ref[pl.ds(pad, rows), :] = u

    step = lax.broadcasted_iota(jnp.int32, (rows, 1), 0) // stride
    if zero_first:
        step = step + pl.program_id(1) * (rows // stride)
    pos = step + start_pos

    ys = []
    for g, w in enumerate(POOL_WINDOWS):
        cols = slice(g * POOL_GROUP, (g + 1) * POOL_GROUP)
        acc = ext_ref[:, cols]
        span = 1
        while span < w:
            acc = acc + pltpu.roll(acc, span * stride, axis=0)
            span *= 2
        win_sum = acc[pad:, :]
        count = jnp.minimum(pos + 1, w).astype(_F32)
        diff = win_sum / count - u[:, cols]
        ys.append(_dot(diff.astype(_BF16), wgrp_ref[g]))
    y = jnp.concatenate(ys, axis=-1) * scale_ref[...]
    o_ref[...] = x + _rmsnorm(y, gpost_ref[...])

    tail = ext_ref[pl.ds(rows, pad), :]
    newhist_ref[...] = tail.reshape(newhist_ref.shape)
    ext_ref[pl.ds(0, pad), :] = tail


def _mixer_call(kernel, name, x_all, hist, params, *, n_seq, seq_len,
                stride, pad, row_offset, zero_first, **static):
    if stride == 1:
        rows = min(MIX_ROWS, seq_len)
        blocks_per_seq = seq_len // rows
        grid = (n_seq, blocks_per_seq)
        base = row_offset // rows
        row_map = lambda b, j: (base + b * blocks_per_seq + j, 0)
        newhist_shape = (n_seq, pad, D_MODEL)
        newhist_spec = pl.BlockSpec((1, pad, D_MODEL), lambda b, j: (b, 0, 0))
    else:
        rows = n_seq * seq_len
        grid = (1, 1)
        base = row_offset // rows
        row_map = lambda b, j: (base, 0)
        newhist_shape = (pad, D_MODEL)
        newhist_spec = pl.BlockSpec((pad, D_MODEL), lambda b, j: (0, 0))
    assert row_offset % rows == 0
    row_spec = pl.BlockSpec((rows, D_MODEL), row_map)
    body = functools.partial(kernel, rows=rows, stride=stride, pad=pad,
                             zero_first=zero_first, **static)
    return pl.pallas_call(
        body,
        grid=grid,
        in_specs=[row_spec, _resident(hist.shape)] + [_stacked(a, k) for a, k in params],
        out_specs=[row_spec, newhist_spec],
        out_shape=[jax.ShapeDtypeStruct(x_all.shape, _F32),
                   jax.ShapeDtypeStruct(newhist_shape, _F32)],
        scratch_shapes=[pltpu.VMEM((pad + rows, D_MODEL), _F32)],
        input_output_aliases={0: 0},
        compiler_params=pltpu.CompilerParams(
            dimension_semantics=("arbitrary", "arbitrary"),
            vmem_limit_bytes=VMEM_LIMIT_BYTES),
        name=name,
    )(x_all, hist, *[a for a, _ in params])


def kernel(x_prompt, x_sample, state_conv, state_pool, norm_gains, ffn_w_gate, ffn_w_up,
           ffn_w_down, conv_w_in, conv_kernel, conv_w_out, pool_w_group, pool_scale):
    batch, seq, _ = x_prompt.shape
    dec_batch, dec_seq, _ = x_sample.shape
    n_prompt = batch * seq
    n_sample = dec_batch * dec_seq
    n_conv = conv_w_in.shape[0]
    n_pool = pool_w_group.shape[0]

    xp_rows = x_prompt.reshape(n_prompt, D_MODEL)
    xs_rows = jnp.swapaxes(x_sample, 0, 1).reshape(n_sample, D_MODEL)

    gains = norm_gains.reshape(DEPTH * N_NORMS, 1, D_MODEL)
    ffn_w = (ffn_w_gate.reshape(2 * DEPTH, D_MODEL, D_FF),
             ffn_w_up.reshape(2 * DEPTH, D_MODEL, D_FF),
             ffn_w_down.reshape(2 * DEPTH, D_FF, D_MODEL))
    w_in = conv_w_in.astype(_BF16)
    w_out = conv_w_out.astype(_BF16)
    w_grp = pool_w_group.astype(_BF16)
    scale = pool_scale.reshape(n_pool, 1, D_MODEL)

    conv_pad_p = SUBLANES
    conv_pad_s = (CONV_W - 1) * dec_batch
    pool_pad_p = POOL_PAD_STEPS
    pool_pad_s = POOL_PAD_STEPS * dec_batch
    dummy_hist = jnp.zeros((SUBLANES, D_MODEL), _F32)
    conv_hist_s = jnp.swapaxes(state_conv, 1, 2).reshape(n_conv, conv_pad_s, D_MODEL)
    pool_hist_s = jnp.pad(jnp.swapaxes(state_pool, 1, 2),
                          ((0, 0), (POOL_PAD_STEPS - POOL_HIST, 0), (0, 0), (0, 0)))
    pool_hist_s = pool_hist_s.reshape(n_pool, pool_pad_s, D_MODEL)
    ffn = functools.partial(_ffn_sublayer, n_prompt=n_prompt, n_sample=n_sample)

    new_conv_p, new_conv_s, new_pool_p, new_pool_s = [], [], [], []
    for i in range(DEPTH):
        g0 = i * N_NORMS
        if i == 0:
            x_all = ffn((xp_rows, xs_rows), gains, g0, ffn_w, 2 * i, mode="first")
        else:
            x_all = ffn((x_all,), gains, g0, ffn_w, 2 * i, mode="mid")
        j = i // 2
        norms = [(gains, g0 + 2), (gains, g0 + 3)]
        if i % 2 == 0:
            params = norms + [(w_in, j), (conv_kernel, j), (w_out, j)]
            x_all, hp = _mixer_call(
                _conv_kernel, "conv_prompt", x_all, dummy_hist, params,
                n_seq=batch, seq_len=seq, stride=1, pad=conv_pad_p, row_offset=0,
                zero_first=True)
            x_all, hs = _mixer_call(
                _conv_kernel, "conv_sample", x_all, conv_hist_s[j], params,
                n_seq=dec_batch, seq_len=dec_seq, stride=dec_batch, pad=conv_pad_s,
                row_offset=n_prompt, zero_first=False)
            new_conv_p.append(hp[:, conv_pad_p - (CONV_W - 1):, :])
            new_conv_s.append(jnp.swapaxes(
                hs.reshape(CONV_W - 1, dec_batch, D_MODEL), 0, 1))
        else:
            params = norms + [(w_grp, j), (scale, j)]
            x_all, hp = _mixer_call(
                _pool_kernel, "pool_prompt", x_all, dummy_hist, params,
                n_seq=batch, seq_len=seq, stride=1, pad=pool_pad_p, row_offset=0,
                zero_first=True, start_pos=0)
            x_all, hs = _mixer_call(
                _pool_kernel, "pool_sample", x_all, pool_hist_s[j], params,
                n_seq=dec_batch, seq_len=dec_seq, stride=dec_batch, pad=pool_pad_s,
                row_offset=n_prompt, zero_first=False, start_pos=PAST_LEN)
            new_pool_p.append(hp[:, pool_pad_p - POOL_HIST:, :])
            new_pool_s.append(jnp.swapaxes(
                hs.reshape(POOL_PAD_STEPS, dec_batch, D_MODEL)[POOL_PAD_STEPS - POOL_HIST:],
                0, 1))
        if i == DEPTH - 1:
            yp_rows, ys_rows = ffn((x_all,), gains, g0 + 4, ffn_w, 2 * i + 1, mode="last")
        else:
            x_all = ffn((x_all,), gains, g0 + 4, ffn_w, 2 * i + 1, mode="mid")

    y_prompt = yp_rows.reshape(batch, seq, D_MODEL)
    y_sample = jnp.swapaxes(ys_rows.reshape(dec_seq, dec_batch, D_MODEL), 0, 1)
    return (y_prompt, y_sample, jnp.stack(new_conv_p), jnp.stack(new_conv_s),
            jnp.stack(new_pool_p), jnp.stack(new_pool_s))
```

```python
import functools

import jax
import jax.numpy as jnp
from jax import lax
from jax.experimental import pallas as pl
from jax.experimental.pallas import tpu as pltpu

D_MODEL = 1024
D_FF = 2816
DEPTH = 4
N_NORMS = 6
CONV_W = 3
POOL_WINDOWS = (2, 4, 8, 16)
POOL_GROUP = D_MODEL // len(POOL_WINDOWS)
POOL_HIST = max(POOL_WINDOWS) - 1
PAST_LEN = 16384
EPS = 1e-6

SUBLANES = 8
FFN_ROWS = 512
MIX_ROWS = 512
CONV_PAD_ROWS = SUBLANES
POOL_PAD_ROWS = 2 * SUBLANES
VMEM_LIMIT_BYTES = 56 * 1024 * 1024

_F32 = jnp.float32
_BF16 = jnp.bfloat16


def _rmsnorm(x, g):
    ms = jnp.mean(x * x, axis=-1, keepdims=True)
    return x * lax.rsqrt(ms + EPS) * g


def _dot(a, b):
    return jnp.dot(a, b, preferred_element_type=_F32)


def _stacked(arr, idx):
    tail = arr.shape[1:]
    return pl.BlockSpec((None,) + tail, lambda *_: (idx,) + (0,) * len(tail),
                        pipeline_mode=pl.Buffered(1))


def _ffn_kernel(*refs, mode, n_prompt_blocks):
    if mode == "first":
        xp_ref, xs_ref, gpre_ref, gpost_ref, wg_ref, wu_ref, wd_ref, o_ref = refs
    elif mode == "last":
        x_ref, gpre_ref, gpost_ref, wg_ref, wu_ref, wd_ref, yp_ref, ys_ref = refs
    else:
        x_ref, gpre_ref, gpost_ref, wg_ref, wu_ref, wd_ref, o_ref = refs
    is_prompt = pl.program_id(0) < n_prompt_blocks

    if mode == "first":
        x = jnp.where(is_prompt, xp_ref[...], xs_ref[...])
    else:
        x = x_ref[...]
    h = _rmsnorm(x, gpre_ref[...])
    a = _dot(h, wg_ref[...])
    b = _dot(h, wu_ref[...])
    g = a * jax.nn.sigmoid(a) * b
    y = _dot(g, wd_ref[...])
    out = x + _rmsnorm(y, 0.5 * gpost_ref[...])
    if mode == "last":
        ys_ref[...] = out

        @pl.when(is_prompt)
        def _():
            yp_ref[...] = ys_ref[...]
    else:
        o_ref[...] = out


def _ffn_sublayer(xs, gains, g_idx, weights, w_idx, *, mode, n_prompt, n_sample):
    n = n_prompt + n_sample
    rows = FFN_ROWS
    assert n_prompt % rows == 0 and n_sample == rows
    npb = n_prompt // rows
    row_spec = pl.BlockSpec((rows, D_MODEL), lambda i: (i, 0))
    prompt_spec = pl.BlockSpec((rows, D_MODEL), lambda i: (jnp.minimum(i, npb - 1), 0))
    sample_spec = pl.BlockSpec((rows, D_MODEL), lambda i: (0, 0))
    x_all_shape = jax.ShapeDtypeStruct((n, D_MODEL), _F32)
    if mode == "first":
        x_specs, out_specs, out_shape, aliases = [prompt_spec, sample_spec], row_spec, x_all_shape, {}
    elif mode == "last":
        x_specs, aliases = [row_spec], {}
        out_specs = [prompt_spec, sample_spec]
        out_shape = [jax.ShapeDtypeStruct((n_prompt, D_MODEL), _F32),
                     jax.ShapeDtypeStruct((n_sample, D_MODEL), _F32)]
    else:
        x_specs, out_specs, out_shape, aliases = [row_spec], row_spec, x_all_shape, {0: 0}
    wg, wu, wd = weights
    return pl.pallas_call(
        functools.partial(_ffn_kernel, mode=mode, n_prompt_blocks=npb),
        grid=(n // rows,),
        in_specs=x_specs + [_stacked(gains, g_idx), _stacked(gains, g_idx + 1),
                            _stacked(wg, w_idx), _stacked(wu, w_idx), _stacked(wd, w_idx)],
        out_specs=out_specs,
        out_shape=out_shape,
        input_output_aliases=aliases,
        compiler_params=pltpu.CompilerParams(
            dimension_semantics=("arbitrary",), vmem_limit_bytes=VMEM_LIMIT_BYTES),
        name="ffn_" + mode,
    )(*xs, gains, gains, wg, wu, wd)


def _conv_body(x_ref, o_ref, newhist_ref, ext_ref, gpre_ref, gpost_ref, win_ref, ck_ref,
               wout_ref, *, rows, stride, pad, first_step, start_pos):
    del first_step, start_pos
    x = x_ref[...]
    h = _rmsnorm(x, gpre_ref[...]).astype(_BF16)
    gate_c = _dot(h, win_ref[:, D_MODEL:2 * D_MODEL])
    v = _dot(h, win_ref[:, 2 * D_MODEL:])
    z = gate_c * v
    ext_ref[pl.ds(pad, rows), :] = z
    ck = ck_ref[...]
    conv = (ck[0:1] * ext_ref[pl.ds(pad - 2 * stride, rows), :]
            + ck[1:2] * ext_ref[pl.ds(pad - stride, rows), :]
            + ck[2:3] * z)
    gate_b = _dot(h, win_ref[:, :D_MODEL])
    y = _dot((gate_b * conv).astype(_BF16), wout_ref[...])
    o_ref[...] = x + _rmsnorm(y, gpost_ref[...])

    tail = ext_ref[pl.ds(rows, pad), :]
    newhist_ref[...] = tail.reshape(newhist_ref.shape)
    ext_ref[pl.ds(0, pad), :] = tail


def _pool_body(x_ref, o_ref, newhist_ref, ext_ref, gpre_ref, gpost_ref, wgrp_ref, scale_ref,
               *, rows, stride, pad, first_step, start_pos):
    x = x_ref[...]
    u = _rmsnorm(x, gpre_ref[...])
    ext_ref[pl.ds(pad, rows), :] = u

    step = lax.broadcasted_iota(jnp.int32, (rows, 1), 0) // stride
    pos = step + first_step + start_pos

    ys = []
    for g, w in enumerate(POOL_WINDOWS):
        cols = slice(g * POOL_GROUP, (g + 1) * POOL_GROUP)
        acc = ext_ref[:, cols]
        span = 1
        while span < w:
            acc = acc + pltpu.roll(acc, span * stride, axis=0)
            span *= 2
        win_sum = acc[pad:, :]
        count = jnp.minimum(pos + 1, w).astype(_F32)
        diff = win_sum / count - u[:, cols]
        ys.append(_dot(diff.astype(_BF16), wgrp_ref[g]))
    y = jnp.concatenate(ys, axis=-1) * scale_ref[...]
    o_ref[...] = x + _rmsnorm(y, gpost_ref[...])

    tail = ext_ref[pl.ds(rows, pad), :]
    newhist_ref[...] = tail.reshape(newhist_ref.shape)
    ext_ref[pl.ds(0, pad), :] = tail


def _mixer_kernel(x_ref, hist_ref, *refs, body, n_params, rows, blocks_per_seq,
                  n_prompt_blocks, pad_p, pad_s, stride_s):
    params = refs[:n_params]
    o_ref, newhist_p_ref, newhist_s_ref, ext_p_ref, ext_s_ref = refs[n_params:]
    i = pl.program_id(0)

    @pl.when(i < n_prompt_blocks)
    def _():
        j = lax.rem(i, jnp.int32(blocks_per_seq))

        @pl.when(j == 0)
        def _():
            ext_p_ref[pl.ds(0, pad_p), :] = jnp.zeros((pad_p, D_MODEL), _F32)

        body(x_ref, o_ref, newhist_p_ref, ext_p_ref, *params, rows=rows, stride=1, pad=pad_p,
             first_step=j * rows, start_pos=0)

    @pl.when(i == n_prompt_blocks)
    def _():
        ext_s_ref[pl.ds(0, pad_s), :] = hist_ref[...]
        body(x_ref, o_ref, newhist_s_ref, ext_s_ref, *params, rows=rows, stride=stride_s,
             pad=pad_s, first_step=0, start_pos=PAST_LEN)


def _mixer_sublayer(body, name, x_all, hist_s, h_idx, params, *, n_seq, seq_len, n_sample,
                    stride_s, pad_p, pad_s):
    rows = MIX_ROWS
    assert seq_len % rows == 0 and n_sample == rows and hist_s.shape[1] == pad_s
    blocks_per_seq = seq_len // rows
    npb = n_seq * blocks_per_seq
    row_spec = pl.BlockSpec((rows, D_MODEL), lambda i: (i, 0))
    kernel = functools.partial(
        _mixer_kernel, body=body, n_params=len(params), rows=rows, blocks_per_seq=blocks_per_seq,
        n_prompt_blocks=npb, pad_p=pad_p, pad_s=pad_s, stride_s=stride_s)
    return pl.pallas_call(
        kernel,
        grid=(npb + 1,),
        in_specs=[row_spec, _stacked(hist_s, h_idx)] + [_stacked(a, k) for a, k in params],
        out_specs=[row_spec,
                   pl.BlockSpec((1, pad_p, D_MODEL),
                                lambda i: (jnp.minimum(i, npb - 1) // blocks_per_seq, 0, 0)),
                   pl.BlockSpec((pad_s, D_MODEL), lambda i: (0, 0))],
        out_shape=[jax.ShapeDtypeStruct(x_all.shape, _F32),
                   jax.ShapeDtypeStruct((n_seq, pad_p, D_MODEL), _F32),
                   jax.ShapeDtypeStruct((pad_s, D_MODEL), _F32)],
        scratch_shapes=[pltpu.VMEM((pad_p + rows, D_MODEL), _F32),
                        pltpu.VMEM((pad_s + rows, D_MODEL), _F32)],
        input_output_aliases={0: 0},
        compiler_params=pltpu.CompilerParams(
            dimension_semantics=("arbitrary",), vmem_limit_bytes=VMEM_LIMIT_BYTES),
        name=name,
    )(x_all, hist_s, *[a for a, _ in params])


def kernel(x_prompt, x_sample, state_conv, state_pool, norm_gains, ffn_w_gate, ffn_w_up,
           ffn_w_down, conv_w_in, conv_kernel, conv_w_out, pool_w_group, pool_scale):
    batch, seq, _ = x_prompt.shape
    dec_batch, dec_seq, _ = x_sample.shape
    n_prompt = batch * seq
    n_sample = dec_batch * dec_seq
    n_conv = conv_w_in.shape[0]
    n_pool = pool_w_group.shape[0]

    xp_rows = x_prompt.reshape(n_prompt, D_MODEL)
    xs_rows = jnp.swapaxes(x_sample, 0, 1).reshape(n_sample, D_MODEL)

    gains = norm_gains.reshape(DEPTH * N_NORMS, 1, D_MODEL)
    ffn_w = (ffn_w_gate.reshape(2 * DEPTH, D_MODEL, D_FF),
             ffn_w_up.reshape(2 * DEPTH, D_MODEL, D_FF),
             ffn_w_down.reshape(2 * DEPTH, D_FF, D_MODEL))
    w_in = conv_w_in.astype(_BF16)
    w_out = conv_w_out.astype(_BF16)
    w_grp = pool_w_group.astype(_BF16)
    scale = pool_scale.reshape(n_pool, 1, D_MODEL)

    conv_pad_s = (CONV_W - 1) * dec_batch
    pool_pad_s = POOL_HIST * dec_batch
    conv_hist_s = jnp.swapaxes(state_conv, 1, 2).reshape(n_conv, conv_pad_s, D_MODEL)
    pool_hist_s = jnp.swapaxes(state_pool, 1, 2).reshape(n_pool, pool_pad_s, D_MODEL)
    ffn = functools.partial(_ffn_sublayer, n_prompt=n_prompt, n_sample=n_sample)
    mixer = functools.partial(_mixer_sublayer, n_seq=batch, seq_len=seq, n_sample=n_sample,
                              stride_s=dec_batch)

    new_conv_p, new_conv_s, new_pool_p, new_pool_s = [], [], [], []
    for i in range(DEPTH):
        g0 = i * N_NORMS
        if i == 0:
            x_all = ffn((xp_rows, xs_rows), gains, g0, ffn_w, 2 * i, mode="first")
        else:
            x_all = ffn((x_all,), gains, g0, ffn_w, 2 * i, mode="mid")
        j = i // 2
        norms = [(gains, g0 + 2), (gains, g0 + 3)]
        if i % 2 == 0:
            x_all, hp, hs = mixer(
                _conv_body, "conv_mixer", x_all, conv_hist_s, j,
                norms + [(w_in, j), (conv_kernel, j), (w_out, j)],
                pad_p=CONV_PAD_ROWS, pad_s=conv_pad_s)
            new_conv_p.append(hp[:, CONV_PAD_ROWS - (CONV_W - 1):, :])
            new_conv_s.append(jnp.swapaxes(hs.reshape(CONV_W - 1, dec_batch, D_MODEL), 0, 1))
        else:
            x_all, hp, hs = mixer(
                _pool_body, "pool_mixer", x_all, pool_hist_s, j,
                norms + [(w_grp, j), (scale, j)],
                pad_p=POOL_PAD_ROWS, pad_s=pool_pad_s)
            new_pool_p.append(hp[:, POOL_PAD_ROWS - POOL_HIST:, :])
            new_pool_s.append(jnp.swapaxes(hs.reshape(POOL_HIST, dec_batch, D_MODEL), 0, 1))
        if i == DEPTH - 1:
            yp_rows, ys_rows = ffn((x_all,), gains, g0 + 4, ffn_w, 2 * i + 1, mode="last")
        else:
            x_all = ffn((x_all,), gains, g0 + 4, ffn_w, 2 * i + 1, mode="mid")

    y_prompt = yp_rows.reshape(batch, seq, D_MODEL)
    y_sample = jnp.swapaxes(ys_rows.reshape(dec_seq, dec_batch, D_MODEL), 0, 1)
    return (y_prompt, y_sample, jnp.stack(new_conv_p), jnp.stack(new_conv_s),
            jnp.stack(new_pool_p), jnp.stack(new_pool_s))
```

```python
import functools

import jax
import jax.numpy as jnp
from jax import lax
from jax.experimental import pallas as pl
from jax.experimental.pallas import tpu as pltpu

D_MODEL = 1024
D_FF = 2816
DEPTH = 4
N_NORMS = 6
CONV_W = 3
POOL_WINDOWS = (2, 4, 8, 16)
POOL_GROUP = D_MODEL // len(POOL_WINDOWS)
POOL_HIST = max(POOL_WINDOWS) - 1
PAST_LEN = 16384
EPS = 1e-6

SUBLANES = 8
FFN_ROWS = 512
MIX_ROWS = 512
CONV_PAD_ROWS = SUBLANES
POOL_PAD_ROWS = 2 * SUBLANES
VMEM_LIMIT_BYTES = 56 * 1024 * 1024
FFN_VMEM_LIMIT_BYTES = 40 * 1024 * 1024

_F32 = jnp.float32
_BF16 = jnp.bfloat16


def _rmsnorm(x, g):
    ms = jnp.mean(x * x, axis=-1, keepdims=True)
    return x * lax.rsqrt(ms + EPS) * g


def _dot(a, b):
    return jnp.dot(a, b, preferred_element_type=_F32)


def _stacked(arr, idx):
    tail = arr.shape[1:]
    return pl.BlockSpec((None,) + tail, lambda *_: (idx,) + (0,) * len(tail),
                        pipeline_mode=pl.Buffered(1))


def _ffn_kernel(*refs, mode, n_prompt_blocks):
    if mode == "first":
        xp_ref, xs_ref, gpre_ref, gpost_ref, wg_ref, wu_ref, wd_ref, o_ref = refs
    elif mode == "last":
        x_ref, gpre_ref, gpost_ref, wg_ref, wu_ref, wd_ref, yp_ref, ys_ref = refs
    else:
        x_ref, gpre_ref, gpost_ref, wg_ref, wu_ref, wd_ref, o_ref = refs
    is_prompt = pl.program_id(0) < n_prompt_blocks

    if mode == "first":
        x = jnp.where(is_prompt, xp_ref[...], xs_ref[...])
    else:
        x = x_ref[...]
    h = _rmsnorm(x, gpre_ref[...])
    a = _dot(h, wg_ref[...])
    b = _dot(h, wu_ref[...])
    g = a * jax.nn.sigmoid(a) * b
    y = _dot(g, wd_ref[...])
    out = x + _rmsnorm(y, 0.5 * gpost_ref[...])
    if mode == "last":
        ys_ref[...] = out

        @pl.when(is_prompt)
        def _():
            yp_ref[...] = ys_ref[...]
    else:
        o_ref[...] = out


def _ffn_sublayer(xs, gains, g_idx, weights, w_idx, *, mode, n_prompt, n_sample):
    n = n_prompt + n_sample
    rows = FFN_ROWS
    assert n_prompt % rows == 0 and n_sample == rows
    npb = n_prompt // rows
    row_spec = pl.BlockSpec((rows, D_MODEL), lambda i: (i, 0))
    prompt_spec = pl.BlockSpec((rows, D_MODEL), lambda i: (jnp.minimum(i, npb - 1), 0))
    sample_spec = pl.BlockSpec((rows, D_MODEL), lambda i: (0, 0))
    x_all_shape = jax.ShapeDtypeStruct((n, D_MODEL), _F32)
    if mode == "first":
        x_specs, out_specs, out_shape, aliases = [prompt_spec, sample_spec], row_spec, x_all_shape, {}
    elif mode == "last":
        x_specs, aliases = [row_spec], {}
        out_specs = [prompt_spec, sample_spec]
        out_shape = [jax.ShapeDtypeStruct((n_prompt, D_MODEL), _F32),
                     jax.ShapeDtypeStruct((n_sample, D_MODEL), _F32)]
    else:
        x_specs, out_specs, out_shape, aliases = [row_spec], row_spec, x_all_shape, {0: 0}
    wg, wu, wd = weights
    return pl.pallas_call(
        functools.partial(_ffn_kernel, mode=mode, n_prompt_blocks=npb),
        grid=(n // rows,),
        in_specs=x_specs + [_stacked(gains, g_idx), _stacked(gains, g_idx + 1),
                            _stacked(wg, w_idx), _stacked(wu, w_idx),
                            pl.BlockSpec(memory_space=pltpu.VMEM)],
        out_specs=out_specs,
        out_shape=out_shape,
        input_output_aliases=aliases,
        compiler_params=pltpu.CompilerParams(
            dimension_semantics=("arbitrary",), vmem_limit_bytes=FFN_VMEM_LIMIT_BYTES),
        name="ffn_" + mode,
    )(*xs, gains, gains, wg, wu, wd[w_idx])


def _conv_body(x_ref, o_ref, newhist_ref, ext_ref, gpre_ref, gpost_ref, win_ref, ck_ref,
               wout_ref, *, rows, stride, pad, first_step, start_pos):
    del first_step, start_pos
    x = x_ref[...]
    h = _rmsnorm(x, gpre_ref[...]).astype(_BF16)
    gate_c = _dot(h, win_ref[:, D_MODEL:2 * D_MODEL])
    v = _dot(h, win_ref[:, 2 * D_MODEL:])
    z = gate_c * v
    ext_ref[pl.ds(pad, rows), :] = z
    ck = ck_ref[...]
    conv = (ck[0:1] * ext_ref[pl.ds(pad - 2 * stride, rows), :]
            + ck[1:2] * ext_ref[pl.ds(pad - stride, rows), :]
            + ck[2:3] * z)
    gate_b = _dot(h, win_ref[:, :D_MODEL])
    y = _dot((gate_b * conv).astype(_BF16), wout_ref[...])
    o_ref[...] = x + _rmsnorm(y, gpost_ref[...])

    tail = ext_ref[pl.ds(rows, pad), :]
    newhist_ref[...] = tail.reshape(newhist_ref.shape)
    ext_ref[pl.ds(0, pad), :] = tail


def _pool_body(x_ref, o_ref, newhist_ref, ext_ref, gpre_ref, gpost_ref, wgrp_ref, scale_ref,
               *, rows, stride, pad, first_step, start_pos):
    x = x_ref[...]
    u = _rmsnorm(x, gpre_ref[...])
    ext_ref[pl.ds(pad, rows), :] = u

    step = lax.broadcasted_iota(jnp.int32, (rows, 1), 0) // stride
    pos = step + first_step + start_pos

    ys = []
    for g, w in enumerate(POOL_WINDOWS):
        cols = slice(g * POOL_GROUP, (g + 1) * POOL_GROUP)
        acc = ext_ref[:, cols]
        span = 1
        while span < w:
            acc = acc + pltpu.roll(acc, span * stride, axis=0)
            span *= 2
        win_sum = acc[pad:, :]
        count = jnp.minimum(pos + 1, w).astype(_F32)
        diff = win_sum / count - u[:, cols]
        ys.append(_dot(diff.astype(_BF16), wgrp_ref[g]))
    y = jnp.concatenate(ys, axis=-1) * scale_ref[...]
    o_ref[...] = x + _rmsnorm(y, gpost_ref[...])

    tail = ext_ref[pl.ds(rows, pad), :]
    newhist_ref[...] = tail.reshape(newhist_ref.shape)
    ext_ref[pl.ds(0, pad), :] = tail


def _mixer_kernel(x_ref, hist_ref, *refs, body, n_params, rows, blocks_per_seq,
                  n_prompt_blocks, pad_p, pad_s, stride_s):
    params = refs[:n_params]
    o_ref, newhist_p_ref, newhist_s_ref, ext_p_ref, ext_s_ref = refs[n_params:]
    i = pl.program_id(0)

    @pl.when(i < n_prompt_blocks)
    def _():
        j = lax.rem(i, jnp.int32(blocks_per_seq))

        @pl.when(j == 0)
        def _():
            ext_p_ref[pl.ds(0, pad_p), :] = jnp.zeros((pad_p, D_MODEL), _F32)

        body(x_ref, o_ref, newhist_p_ref, ext_p_ref, *params, rows=rows, stride=1, pad=pad_p,
             first_step=j * rows, start_pos=0)

    @pl.when(i == n_prompt_blocks)
    def _():
        ext_s_ref[pl.ds(0, pad_s), :] = hist_ref[...]
        body(x_ref, o_ref, newhist_s_ref, ext_s_ref, *params, rows=rows, stride=stride_s,
             pad=pad_s, first_step=0, start_pos=PAST_LEN)


def _mixer_sublayer(body, name, x_all, hist_s, h_idx, params, *, n_seq, seq_len, n_sample,
                    stride_s, pad_p, pad_s):
    rows = MIX_ROWS
    assert seq_len % rows == 0 and n_sample == rows and hist_s.shape[1] == pad_s
    blocks_per_seq = seq_len // rows
    npb = n_seq * blocks_per_seq
    row_spec = pl.BlockSpec((rows, D_MODEL), lambda i: (i, 0))
    kernel = functools.partial(
        _mixer_kernel, body=body, n_params=len(params), rows=rows, blocks_per_seq=blocks_per_seq,
        n_prompt_blocks=npb, pad_p=pad_p, pad_s=pad_s, stride_s=stride_s)
    return pl.pallas_call(
        kernel,
        grid=(npb + 1,),
        in_specs=[row_spec, _stacked(hist_s, h_idx)] + [_stacked(a, k) for a, k in params],
        out_specs=[row_spec,
                   pl.BlockSpec((1, pad_p, D_MODEL),
                                lambda i: (jnp.minimum(i, npb - 1) // blocks_per_seq, 0, 0)),
                   pl.BlockSpec((pad_s, D_MODEL), lambda i: (0, 0))],
        out_shape=[jax.ShapeDtypeStruct(x_all.shape, _F32),
                   jax.ShapeDtypeStruct((n_seq, pad_p, D_MODEL), _F32),
                   jax.ShapeDtypeStruct((pad_s, D_MODEL), _F32)],
        scratch_shapes=[pltpu.VMEM((pad_p + rows, D_MODEL), _F32),
                        pltpu.VMEM((pad_s + rows, D_MODEL), _F32)],
        input_output_aliases={0: 0},
        compiler_params=pltpu.CompilerParams(
            dimension_semantics=("arbitrary",), vmem_limit_bytes=VMEM_LIMIT_BYTES),
        name=name,
    )(x_all, hist_s, *[a for a, _ in params])


def kernel(x_prompt, x_sample, state_conv, state_pool, norm_gains, ffn_w_gate, ffn_w_up,
           ffn_w_down, conv_w_in, conv_kernel, conv_w_out, pool_w_group, pool_scale):
    batch, seq, _ = x_prompt.shape
    dec_batch, dec_seq, _ = x_sample.shape
    n_prompt = batch * seq
    n_sample = dec_batch * dec_seq
    n_conv = conv_w_in.shape[0]
    n_pool = pool_w_group.shape[0]

    xp_rows = x_prompt.reshape(n_prompt, D_MODEL)
    xs_rows = jnp.swapaxes(x_sample, 0, 1).reshape(n_sample, D_MODEL)

    gains = norm_gains.reshape(DEPTH * N_NORMS, 1, D_MODEL)
    ffn_w = (ffn_w_gate.reshape(2 * DEPTH, D_MODEL, D_FF),
             ffn_w_up.reshape(2 * DEPTH, D_MODEL, D_FF),
             ffn_w_down.reshape(2 * DEPTH, D_FF, D_MODEL))
    w_in = conv_w_in.astype(_BF16)
    w_out = conv_w_out.astype(_BF16)
    w_grp = pool_w_group.astype(_BF16)
    scale = pool_scale.reshape(n_pool, 1, D_MODEL)

    conv_pad_s = (CONV_W - 1) * dec_batch
    pool_pad_s = POOL_HIST * dec_batch
    conv_hist_s = jnp.swapaxes(state_conv, 1, 2).reshape(n_conv, conv_pad_s, D_MODEL)
    pool_hist_s = jnp.swapaxes(state_pool, 1, 2).reshape(n_pool, pool_pad_s, D_MODEL)
    ffn = functools.partial(_ffn_sublayer, n_prompt=n_prompt, n_sample=n_sample)
    mixer = functools.partial(_mixer_sublayer, n_seq=batch, seq_len=seq, n_sample=n_sample,
                              stride_s=dec_batch)

    new_conv_p, new_conv_s, new_pool_p, new_pool_s = [], [], [], []
    for i in range(DEPTH):
        g0 = i * N_NORMS
        if i == 0:
            x_all = ffn((xp_rows, xs_rows), gains, g0, ffn_w, 2 * i, mode="first")
        else:
            x_all = ffn((x_all,), gains, g0, ffn_w, 2 * i, mode="mid")
        j = i // 2
        norms = [(gains, g0 + 2), (gains, g0 + 3)]
        if i % 2 == 0:
            x_all, hp, hs = mixer(
                _conv_body, "conv_mixer", x_all, conv_hist_s, j,
                norms + [(w_in, j), (conv_kernel, j), (w_out, j)],
                pad_p=CONV_PAD_ROWS, pad_s=conv_pad_s)
            new_conv_p.append(hp[:, CONV_PAD_ROWS - (CONV_W - 1):, :])
            new_conv_s.append(jnp.swapaxes(hs.reshape(CONV_W - 1, dec_batch, D_MODEL), 0, 1))
        else:
            x_all, hp, hs = mixer(
                _pool_body, "pool_mixer", x_all, pool_hist_s, j,
                norms + [(w_grp, j), (scale, j)],
                pad_p=POOL_PAD_ROWS, pad_s=pool_pad_s)
            new_pool_p.append(hp[:, POOL_PAD_ROWS - POOL_HIST:, :])
            new_pool_s.append(jnp.swapaxes(hs.reshape(POOL_HIST, dec_batch, D_MODEL), 0, 1))
        if i == DEPTH - 1:
            yp_rows, ys_rows = ffn((x_all,), gains, g0 + 4, ffn_w, 2 * i + 1, mode="last")
        else:
            x_all = ffn((x_all,), gains, g0 + 4, ffn_w, 2 * i + 1, mode="mid")

    y_prompt = yp_rows.reshape(batch, seq, D_MODEL)
    y_sample = jnp.swapaxes(ys_rows.reshape(dec_seq, dec_batch, D_MODEL), 0, 1)
    return (y_prompt, y_sample, jnp.stack(new_conv_p), jnp.stack(new_conv_s),
            jnp.stack(new_pool_p), jnp.stack(new_pool_s))
```

```python
import functools

import jax
import jax.numpy as jnp
from jax import lax
from jax.experimental import pallas as pl
from jax.experimental.pallas import tpu as pltpu

D_MODEL = 1024
D_FF = 2816
DEPTH = 4
N_NORMS = 6
CONV_W = 3
POOL_WINDOWS = (2, 4, 8, 16)
POOL_GROUP = D_MODEL // len(POOL_WINDOWS)
POOL_HIST = max(POOL_WINDOWS) - 1
PAST_LEN = 16384
EPS = 1e-6

SUBLANES = 8
FFN_ROWS = 512
FFN_SUB_ROWS = 256
CAST_ROWS_IN = 32
CAST_ROWS_DOWN = 176
MIX_ROWS = 512
CONV_PAD_ROWS = SUBLANES
POOL_PAD_ROWS = 2 * SUBLANES
VMEM_LIMIT_BYTES = 56 * 1024 * 1024

_F32 = jnp.float32
_BF16 = jnp.bfloat16


def _rmsnorm(x, g):
    ms = jnp.mean(x * x, axis=-1, keepdims=True)
    return x * lax.rsqrt(ms + EPS) * g


def _dot(a, b):
    return jnp.dot(a, b, preferred_element_type=_F32)


def _stacked(arr, idx):
    tail = arr.shape[1:]
    return pl.BlockSpec((None,) + tail, lambda *_: (idx,) + (0,) * len(tail),
                        pipeline_mode=pl.Buffered(1))


def _cast_side_job(i, last, next_idx, src_hbm, dst_hbm, stage, ostage, sem_in, sem_out):
    slot = i & 1
    n_mats = len(src_hbm)
    chunk_rows = [st.shape[1] for st in stage]
    n_chunks = [dst.shape[0] // r for dst, r in zip(dst_hbm, chunk_rows)]

    def rows_of(k, c):
        r = chunk_rows[k]
        return pl.ds(pl.multiple_of(c * r, r), r)

    def in_copy(k, c, s):
        return pltpu.make_async_copy(src_hbm[k].at[next_idx, rows_of(k, c)], stage[k].at[s],
                                     sem_in.at[s, k])

    def out_copy(k, c, s):
        return pltpu.make_async_copy(ostage[k].at[s], dst_hbm[k].at[rows_of(k, c)],
                                     sem_out.at[s, k])

    def prologue():
        for k in range(n_mats):
            pl.when(i == 0)(lambda k=k: in_copy(k, 0, 0).start())
            pl.when(i < n_chunks[k])(lambda k=k: in_copy(k, i, slot).wait())
            pl.when(i + 1 < n_chunks[k])(lambda k=k: in_copy(k, i + 1, 1 - slot).start())
            pl.when((i >= 2) & (i - 2 < n_chunks[k]))(lambda k=k: out_copy(k, i - 2, slot).wait())

    def cast():
        for k in range(n_mats):
            ostage[k].at[slot][...] = stage[k].at[slot][...].astype(_BF16)

    def epilogue():
        for k in range(n_mats):
            pl.when(i < n_chunks[k])(lambda k=k: out_copy(k, i, slot).start())
            for c in range(n_chunks[k]):
                if c + 2 > last:
                    assert c <= last
                    pl.when(i == last)(lambda k=k, c=c: out_copy(k, c, c & 1).wait())

    return prologue, cast, epilogue


def _ffn_kernel(*refs, mode, n_prompt_blocks, n_steps, next_idx):
    n_x = 2 if mode == "first" else 1
    n_out = 2 if mode == "last" else 1
    x_refs, refs = refs[:n_x], refs[n_x:]
    (gpre_ref, gpost_ref, wg_ref, wu_ref, wd_ref), refs = refs[:5], refs[5:]
    if next_idx is not None:
        src_hbm, refs = refs[:3], refs[3:]
    out_refs, refs = refs[:n_out], refs[n_out:]
    i = pl.program_id(0)
    is_prompt = i < n_prompt_blocks

    if next_idx is not None:
        dst_hbm, stage, ostage, (sem_in, sem_out) = refs[:3], refs[3:6], refs[6:9], refs[9:]
        prologue, cast, epilogue = _cast_side_job(
            i, n_steps - 1, next_idx, src_hbm, dst_hbm, stage, ostage, sem_in, sem_out)
        prologue()
        cast()

    for s in range(FFN_ROWS // FFN_SUB_ROWS):
        rows = pl.ds(s * FFN_SUB_ROWS, FFN_SUB_ROWS)
        if mode == "first":
            x = jnp.where(is_prompt, x_refs[0][rows, :], x_refs[1][rows, :])
        else:
            x = x_refs[0][rows, :]
        h = _rmsnorm(x, gpre_ref[...]).astype(_BF16)
        a = _dot(h, wg_ref[...])
        b = _dot(h, wu_ref[...])
        g = (a * jax.nn.sigmoid(a) * b).astype(_BF16)
        y = _dot(g, wd_ref[...])
        out = x + _rmsnorm(y, 0.5 * gpost_ref[...])
        if mode == "last":
            out_refs[1][rows, :] = out
        else:
            out_refs[0][rows, :] = out

    if mode == "last":
        @pl.when(is_prompt)
        def _():
            out_refs[0][...] = out_refs[1][...]

    if next_idx is not None:
        epilogue()


def _ffn_sublayer(xs, gains, g_idx, w16, w32, next_idx, *, mode, n_prompt, n_sample):
    n = n_prompt + n_sample
    rows = FFN_ROWS
    assert n_prompt % rows == 0 and n_sample == rows
    n_steps = n // rows
    npb = n_prompt // rows
    row_spec = pl.BlockSpec((rows, D_MODEL), lambda i: (i, 0))
    prompt_spec = pl.BlockSpec((rows, D_MODEL), lambda i: (jnp.minimum(i, npb - 1), 0))
    sample_spec = pl.BlockSpec((rows, D_MODEL), lambda i: (0, 0))
    any_spec = pl.BlockSpec(memory_space=pl.ANY)
    x_all_shape = jax.ShapeDtypeStruct((n, D_MODEL), _F32)
    if mode == "first":
        x_specs, out_specs, out_shape, aliases = [prompt_spec, sample_spec], [row_spec], [x_all_shape], {}
    elif mode == "last":
        x_specs, aliases = [row_spec], {}
        out_specs = [prompt_spec, sample_spec]
        out_shape = [jax.ShapeDtypeStruct((n_prompt, D_MODEL), _F32),
                     jax.ShapeDtypeStruct((n_sample, D_MODEL), _F32)]
    else:
        x_specs, out_specs, out_shape, aliases = [row_spec], [row_spec], [x_all_shape], {0: 0}
    n_main_out = len(out_specs)
    w_specs = [pl.BlockSpec(w.shape, lambda i: (0, 0), pipeline_mode=pl.Buffered(1)) for w in w16]
    operands = [*xs, gains, gains, *w16]
    in_specs = x_specs + [_stacked(gains, g_idx), _stacked(gains, g_idx + 1)] + w_specs
    scratch = []
    if next_idx is not None:
        operands += list(w32)
        in_specs += [any_spec] * 3
        out_specs = out_specs + [any_spec] * 3
        out_shape = out_shape + [jax.ShapeDtypeStruct(w.shape[1:], _BF16) for w in w32]
        chunk_rows = (CAST_ROWS_IN, CAST_ROWS_IN, CAST_ROWS_DOWN)
        for w, r in zip(w32, chunk_rows):
            assert w.shape[1] % r == 0 and w.shape[1] // r <= n_steps
        scratch = ([pltpu.VMEM((2, r, w.shape[2]), _F32) for w, r in zip(w32, chunk_rows)]
                   + [pltpu.VMEM((2, r, w.shape[2]), _BF16) for w, r in zip(w32, chunk_rows)]
                   + [pltpu.SemaphoreType.DMA((2, 3)), pltpu.SemaphoreType.DMA((2, 3))])
    outs = pl.pallas_call(
        functools.partial(_ffn_kernel, mode=mode, n_prompt_blocks=npb, n_steps=n_steps,
                          next_idx=next_idx),
        grid=(n_steps,),
        in_specs=in_specs,
        out_specs=out_specs,
        out_shape=out_shape,
        scratch_shapes=scratch,
        input_output_aliases=aliases,
        compiler_params=pltpu.CompilerParams(
            dimension_semantics=("arbitrary",), vmem_limit_bytes=VMEM_LIMIT_BYTES),
        name="ffn_" + mode,
    )(*operands)
    return outs[:n_main_out], (tuple(outs[n_main_out:]) if next_idx is not None else None)


def _conv_body(x_ref, o_ref, newhist_ref, ext_ref, gpre_ref, gpost_ref, win_ref, ck_ref,
               wout_ref, *, rows, stride, pad, first_step, start_pos):
    del first_step, start_pos
    x = x_ref[...]
    h = _rmsnorm(x, gpre_ref[...]).astype(_BF16)
    gate_c = _dot(h, win_ref[:, D_MODEL:2 * D_MODEL])
    v = _dot(h, win_ref[:, 2 * D_MODEL:])
    z = gate_c * v
    ext_ref[pl.ds(pad, rows), :] = z
    ck = ck_ref[...]
    conv = (ck[0:1] * ext_ref[pl.ds(pad - 2 * stride, rows), :]
            + ck[1:2] * ext_ref[pl.ds(pad - stride, rows), :]
            + ck[2:3] * z)
    gate_b = _dot(h, win_ref[:, :D_MODEL])
    y = _dot((gate_b * conv).astype(_BF16), wout_ref[...])
    o_ref[...] = x + _rmsnorm(y, gpost_ref[...])

    tail = ext_ref[pl.ds(rows, pad), :]
    newhist_ref[...] = tail.reshape(newhist_ref.shape)
    ext_ref[pl.ds(0, pad), :] = tail


def _pool_body(x_ref, o_ref, newhist_ref, ext_ref, gpre_ref, gpost_ref, wgrp_ref, scale_ref,
               *, rows, stride, pad, first_step, start_pos):
    x = x_ref[...]
    u = _rmsnorm(x, gpre_ref[...])
    ext_ref[pl.ds(pad, rows), :] = u

    step = lax.broadcasted_iota(jnp.int32, (rows, 1), 0) // stride
    pos = step + first_step + start_pos

    ys = []
    for g, w in enumerate(POOL_WINDOWS):
        cols = slice(g * POOL_GROUP, (g + 1) * POOL_GROUP)
        acc = ext_ref[:, cols]
        span = 1
        while span < w:
            acc = acc + pltpu.roll(acc, span * stride, axis=0)
            span *= 2
        win_sum = acc[pad:, :]
        count = jnp.minimum(pos + 1, w).astype(_F32)
        diff = win_sum / count - u[:, cols]
        ys.append(_dot(diff.astype(_BF16), wgrp_ref[g]))
    y = jnp.concatenate(ys, axis=-1) * scale_ref[...]
    o_ref[...] = x + _rmsnorm(y, gpost_ref[...])

    tail = ext_ref[pl.ds(rows, pad), :]
    newhist_ref[...] = tail.reshape(newhist_ref.shape)
    ext_ref[pl.ds(0, pad), :] = tail


def _mixer_kernel(x_ref, hist_ref, *refs, body, n_params, rows, blocks_per_seq,
                  n_prompt_blocks, pad_p, pad_s, stride_s):
    params = refs[:n_params]
    o_ref, newhist_p_ref, newhist_s_ref, ext_p_ref, ext_s_ref = refs[n_params:]
    i = pl.program_id(0)

    @pl.when(i < n_prompt_blocks)
    def _():
        j = lax.rem(i, jnp.int32(blocks_per_seq))

        @pl.when(j == 0)
        def _():
            ext_p_ref[pl.ds(0, pad_p), :] = jnp.zeros((pad_p, D_MODEL), _F32)

        body(x_ref, o_ref, newhist_p_ref, ext_p_ref, *params, rows=rows, stride=1, pad=pad_p,
             first_step=j * rows, start_pos=0)

    @pl.when(i == n_prompt_blocks)
    def _():
        ext_s_ref[pl.ds(0, pad_s), :] = hist_ref[...]
        body(x_ref, o_ref, newhist_s_ref, ext_s_ref, *params, rows=rows, stride=stride_s,
             pad=pad_s, first_step=0, start_pos=PAST_LEN)


def _mixer_sublayer(body, name, x_all, hist_s, h_idx, params, *, n_seq, seq_len, n_sample,
                    stride_s, pad_p, pad_s):
    rows = MIX_ROWS
    assert seq_len % rows == 0 and n_sample == rows and hist_s.shape[1] == pad_s
    blocks_per_seq = seq_len // rows
    npb = n_seq * blocks_per_seq
    row_spec = pl.BlockSpec((rows, D_MODEL), lambda i: (i, 0))
    kernel = functools.partial(
        _mixer_kernel, body=body, n_params=len(params), rows=rows, blocks_per_seq=blocks_per_seq,
        n_prompt_blocks=npb, pad_p=pad_p, pad_s=pad_s, stride_s=stride_s)
    return pl.pallas_call(
        kernel,
        grid=(npb + 1,),
        in_specs=[row_spec, _stacked(hist_s, h_idx)] + [_stacked(a, k) for a, k in params],
        out_specs=[row_spec,
                   pl.BlockSpec((1, pad_p, D_MODEL),
                                lambda i: (jnp.minimum(i, npb - 1) // blocks_per_seq, 0, 0)),
                   pl.BlockSpec((pad_s, D_MODEL), lambda i: (0, 0))],
        out_shape=[jax.ShapeDtypeStruct(x_all.shape, _F32),
                   jax.ShapeDtypeStruct((n_seq, pad_p, D_MODEL), _F32),
                   jax.ShapeDtypeStruct((pad_s, D_MODEL), _F32)],
        scratch_shapes=[pltpu.VMEM((pad_p + rows, D_MODEL), _F32),
                        pltpu.VMEM((pad_s + rows, D_MODEL), _F32)],
        input_output_aliases={0: 0},
        compiler_params=pltpu.CompilerParams(
            dimension_semantics=("arbitrary",), vmem_limit_bytes=VMEM_LIMIT_BYTES),
        name=name,
    )(x_all, hist_s, *[a for a, _ in params])


def kernel(x_prompt, x_sample, state_conv, state_pool, norm_gains, ffn_w_gate, ffn_w_up,
           ffn_w_down, conv_w_in, conv_kernel, conv_w_out, pool_w_group, pool_scale):
    batch, seq, _ = x_prompt.shape
    dec_batch, dec_seq, _ = x_sample.shape
    n_prompt = batch * seq
    n_sample = dec_batch * dec_seq
    n_conv = conv_w_in.shape[0]
    n_pool = pool_w_group.shape[0]

    xp_rows = x_prompt.reshape(n_prompt, D_MODEL)
    xs_rows = jnp.swapaxes(x_sample, 0, 1).reshape(n_sample, D_MODEL)

    gains = norm_gains.reshape(DEPTH * N_NORMS, 1, D_MODEL)
    ffn_w = (ffn_w_gate.reshape(2 * DEPTH, D_MODEL, D_FF),
             ffn_w_up.reshape(2 * DEPTH, D_MODEL, D_FF),
             ffn_w_down.reshape(2 * DEPTH, D_FF, D_MODEL))
    w_in = conv_w_in.astype(_BF16)
    w_out = conv_w_out.astype(_BF16)
    w_grp = pool_w_group.astype(_BF16)
    scale = pool_scale.reshape(n_pool, 1, D_MODEL)

    conv_pad_s = (CONV_W - 1) * dec_batch
    pool_pad_s = POOL_HIST * dec_batch
    conv_hist_s = jnp.swapaxes(state_conv, 1, 2).reshape(n_conv, conv_pad_s, D_MODEL)
    pool_hist_s = jnp.swapaxes(state_pool, 1, 2).reshape(n_pool, pool_pad_s, D_MODEL)
    ffn = functools.partial(_ffn_sublayer, n_prompt=n_prompt, n_sample=n_sample)
    w16 = tuple(w[0].astype(_BF16) for w in ffn_w)
    mixer = functools.partial(_mixer_sublayer, n_seq=batch, seq_len=seq, n_sample=n_sample,
                              stride_s=dec_batch)

    new_conv_p, new_conv_s, new_pool_p, new_pool_s = [], [], [], []
    for i in range(DEPTH):
        g0 = i * N_NORMS
        xs = (xp_rows, xs_rows) if i == 0 else (x_all,)
        (x_all,), w16 = ffn(xs, gains, g0, w16, ffn_w, 2 * i + 1,
                            mode="first" if i == 0 else "mid")
        j = i // 2
        norms = [(gains, g0 + 2), (gains, g0 + 3)]
        if i % 2 == 0:
            x_all, hp, hs = mixer(
                _conv_body, "conv_mixer", x_all, conv_hist_s, j,
                norms + [(w_in, j), (conv_kernel, j), (w_out, j)],
                pad_p=CONV_PAD_ROWS, pad_s=conv_pad_s)
            new_conv_p.append(hp[:, CONV_PAD_ROWS - (CONV_W - 1):, :])
            new_conv_s.append(jnp.swapaxes(hs.reshape(CONV_W - 1, dec_batch, D_MODEL), 0, 1))
        else:
            x_all, hp, hs = mixer(
                _pool_body, "pool_mixer", x_all, pool_hist_s, j,
                norms + [(w_grp, j), (scale, j)],
                pad_p=POOL_PAD_ROWS, pad_s=pool_pad_s)
            new_pool_p.append(hp[:, POOL_PAD_ROWS - POOL_HIST:, :])
            new_pool_s.append(jnp.swapaxes(hs.reshape(POOL_HIST, dec_batch, D_MODEL), 0, 1))
        if i == DEPTH - 1:
            (yp_rows, ys_rows), _ = ffn((x_all,), gains, g0 + 4, w16, ffn_w, None, mode="last")
        else:
            (x_all,), w16 = ffn((x_all,), gains, g0 + 4, w16, ffn_w, 2 * i + 2, mode="mid")

    y_prompt = yp_rows.reshape(batch, seq, D_MODEL)
    y_sample = jnp.swapaxes(ys_rows.reshape(dec_seq, dec_batch, D_MODEL), 0, 1)
    return (y_prompt, y_sample, jnp.stack(new_conv_p), jnp.stack(new_conv_s),
            jnp.stack(new_pool_p), jnp.stack(new_pool_s))
```

```python
import functools

import jax
import jax.numpy as jnp
from jax import lax
from jax.experimental import pallas as pl
from jax.experimental.pallas import tpu as pltpu

D_MODEL = 1024
D_FF = 2816
DEPTH = 4
N_NORMS = 6
CONV_W = 3
POOL_WINDOWS = (2, 4, 8, 16)
POOL_GROUP = D_MODEL // len(POOL_WINDOWS)
POOL_HIST = max(POOL_WINDOWS) - 1
PAST_LEN = 16384
EPS = 1e-6

SUBLANES = 8
FFN_ROWS = 512
FFN_SUB_ROWS = 256
CAST_ROWS_IN = 32
CAST_ROWS_DOWN = 176
MIX_ROWS = 512
CONV_PAD_ROWS = SUBLANES
POOL_PAD_ROWS = 2 * SUBLANES
VMEM_LIMIT_BYTES = 56 * 1024 * 1024

_F32 = jnp.float32
_BF16 = jnp.bfloat16


def _rmsnorm(x, g):
    ms = jnp.mean(x * x, axis=-1, keepdims=True)
    return x * lax.rsqrt(ms + EPS) * g


def _dot(a, b):
    return jnp.dot(a, b, preferred_element_type=_F32)


def _whole(arr):
    nd = arr.ndim
    return pl.BlockSpec(arr.shape, lambda *_: (0,) * nd, pipeline_mode=pl.Buffered(1))


def _stacked(arr, idx):
    if idx is None:
        return _whole(arr)
    tail = arr.shape[1:]
    return pl.BlockSpec((None,) + tail, lambda *_: (idx,) + (0,) * len(tail),
                        pipeline_mode=pl.Buffered(1))


def _cast_side_job(i, last, jobs, dst_hbm, stage, ostage, sem_in, sem_out):
    slot = i & 1
    n_mats = len(jobs)
    chunk_rows = [st.shape[1] for st in stage]
    n_chunks = [dst.shape[0] // r for dst, r in zip(dst_hbm, chunk_rows)]
    by_count = {}
    for k, n in enumerate(n_chunks):
        by_count.setdefault(n, []).append(k)

    def rows_of(k, c):
        r = chunk_rows[k]
        return pl.ds(pl.multiple_of(c * r, r), r)

    def in_copy(k, c, s):
        src, idx = jobs[k]
        return pltpu.make_async_copy(src.at[idx, rows_of(k, c)], stage[k].at[s], sem_in.at[s, k])

    def out_copy(k, c, s):
        return pltpu.make_async_copy(ostage[k].at[s], dst_hbm[k].at[rows_of(k, c)],
                                     sem_out.at[s, k])

    def each(ks, fn):
        def run():
            for k in ks:
                fn(k)
        return run

    def prologue():
        pl.when(i == 0)(each(range(n_mats), lambda k: in_copy(k, 0, 0).start()))
        for n, ks in by_count.items():
            pl.when(i < n)(each(ks, lambda k: in_copy(k, i, slot).wait()))
            pl.when(i + 1 < n)(each(ks, lambda k: in_copy(k, i + 1, 1 - slot).start()))
            pl.when((i >= 2) & (i - 2 < n))(each(ks, lambda k: out_copy(k, i - 2, slot).wait()))

    def cast():
        for k in range(n_mats):
            ostage[k].at[slot][...] = stage[k].at[slot][...].astype(_BF16)

    def epilogue():
        for n, ks in by_count.items():
            pl.when(i < n)(each(ks, lambda k: out_copy(k, i, slot).start()))
            for c in range(max(last - 1, 0), n):
                assert c <= last
                pl.when(i == last)(each(ks, lambda k, c=c: out_copy(k, c, c & 1).wait()))

    return prologue, cast, epilogue


def _ffn_kernel(*refs, mode, n_prompt_blocks, n_steps, cast_idx):
    n_x = 2 if mode == "first" else 1
    n_out = 2 if mode == "last" else 1
    n_cast = len(cast_idx)
    x_refs, refs = refs[:n_x], refs[n_x:]
    (gpre_ref, gpost_ref, wg_ref, wu_ref, wd_ref), refs = refs[:5], refs[5:]
    src_hbm, refs = refs[:n_cast], refs[n_cast:]
    out_refs, refs = refs[:n_out], refs[n_out:]
    i = pl.program_id(0)
    is_prompt = i < n_prompt_blocks

    if n_cast:
        dst_hbm, stage, ostage = refs[:n_cast], refs[n_cast:2 * n_cast], refs[2 * n_cast:3 * n_cast]
        sem_in, sem_out = refs[3 * n_cast:]
        prologue, cast, epilogue = _cast_side_job(
            i, n_steps - 1, list(zip(src_hbm, cast_idx)), dst_hbm, stage, ostage, sem_in, sem_out)
        prologue()
        cast()

    for s in range(FFN_ROWS // FFN_SUB_ROWS):
        rows = pl.ds(s * FFN_SUB_ROWS, FFN_SUB_ROWS)
        if mode == "first":
            x = jnp.where(is_prompt, x_refs[0][rows, :], x_refs[1][rows, :])
        else:
            x = x_refs[0][rows, :]
        h = _rmsnorm(x, gpre_ref[...]).astype(_BF16)
        a = _dot(h, wg_ref[...])
        b = _dot(h, wu_ref[...])
        g = (a * jax.nn.sigmoid(a) * b).astype(_BF16)
        y = _dot(g, wd_ref[...])
        out = x + _rmsnorm(y, 0.5 * gpost_ref[...])
        if mode == "last":
            out_refs[1][rows, :] = out
        else:
            out_refs[0][rows, :] = out

    if mode == "last":
        @pl.when(is_prompt)
        def _():
            out_refs[0][...] = out_refs[1][...]

    if n_cast:
        epilogue()


def _ffn_sublayer(xs, gains, g_idx, w16, cast_jobs, *, mode, n_prompt, n_sample):
    n = n_prompt + n_sample
    rows = FFN_ROWS
    assert n_prompt % rows == 0 and n_sample == rows
    n_steps = n // rows
    npb = n_prompt // rows
    row_spec = pl.BlockSpec((rows, D_MODEL), lambda i: (i, 0))
    prompt_spec = pl.BlockSpec((rows, D_MODEL), lambda i: (jnp.minimum(i, npb - 1), 0))
    sample_spec = pl.BlockSpec((rows, D_MODEL), lambda i: (0, 0))
    any_spec = pl.BlockSpec(memory_space=pl.ANY)
    x_all_shape = jax.ShapeDtypeStruct((n, D_MODEL), _F32)
    if mode == "first":
        x_specs, out_specs, out_shape, aliases = [prompt_spec, sample_spec], [row_spec], [x_all_shape], {}
    elif mode == "last":
        x_specs, aliases = [row_spec], {}
        out_specs = [prompt_spec, sample_spec]
        out_shape = [jax.ShapeDtypeStruct((n_prompt, D_MODEL), _F32),
                     jax.ShapeDtypeStruct((n_sample, D_MODEL), _F32)]
    else:
        x_specs, out_specs, out_shape, aliases = [row_spec], [row_spec], [x_all_shape], {0: 0}
    n_main_out = len(out_specs)
    n_cast = len(cast_jobs)
    scratch = []
    if n_cast:
        for w, _, r in cast_jobs:
            assert w.shape[1] % r == 0 and w.shape[1] // r <= n_steps
        out_specs = out_specs + [any_spec] * n_cast
        out_shape = out_shape + [jax.ShapeDtypeStruct(w.shape[1:], _BF16) for w, _, _ in cast_jobs]
        scratch = ([pltpu.VMEM((2, r, w.shape[2]), _F32) for w, _, r in cast_jobs]
                   + [pltpu.VMEM((2, r, w.shape[2]), _BF16) for w, _, r in cast_jobs]
                   + [pltpu.SemaphoreType.DMA((2, n_cast)), pltpu.SemaphoreType.DMA((2, n_cast))])
    outs = pl.pallas_call(
        functools.partial(_ffn_kernel, mode=mode, n_prompt_blocks=npb, n_steps=n_steps,
                          cast_idx=tuple(idx for _, idx, _ in cast_jobs)),
        grid=(n_steps,),
        in_specs=x_specs + [_stacked(gains, g_idx), _stacked(gains, g_idx + 1)]
        + [_whole(w) for w in w16] + [any_spec] * n_cast,
        out_specs=out_specs,
        out_shape=out_shape,
        scratch_shapes=scratch,
        input_output_aliases=aliases,
        compiler_params=pltpu.CompilerParams(
            dimension_semantics=("arbitrary",), vmem_limit_bytes=VMEM_LIMIT_BYTES),
        name="ffn_" + mode,
    )(*xs, gains, gains, *w16, *[w for w, _, _ in cast_jobs])
    return outs[:n_main_out], list(outs[n_main_out:])


def _conv_body(x_ref, o_ref, newhist_ref, ext_ref, gpre_ref, gpost_ref, win_ref, ck_ref,
               wout_ref, *, rows, stride, pad, first_step, start_pos):
    del first_step, start_pos
    x = x_ref[...]
    h = _rmsnorm(x, gpre_ref[...]).astype(_BF16)
    gate_c = _dot(h, win_ref[:, D_MODEL:2 * D_MODEL])
    v = _dot(h, win_ref[:, 2 * D_MODEL:])
    z = gate_c * v
    ext_ref[pl.ds(pad, rows), :] = z
    ck = ck_ref[...]
    conv = (ck[0:1] * ext_ref[pl.ds(pad - 2 * stride, rows), :]
            + ck[1:2] * ext_ref[pl.ds(pad - stride, rows), :]
            + ck[2:3] * z)
    gate_b = _dot(h, win_ref[:, :D_MODEL])
    y = _dot((gate_b * conv).astype(_BF16), wout_ref[...])
    o_ref[...] = x + _rmsnorm(y, gpost_ref[...])

    tail = ext_ref[pl.ds(rows, pad), :]
    newhist_ref[...] = tail.reshape(newhist_ref.shape)
    ext_ref[pl.ds(0, pad), :] = tail


def _pool_body(x_ref, o_ref, newhist_ref, ext_ref, gpre_ref, gpost_ref, wgrp_ref, scale_ref,
               *, rows, stride, pad, first_step, start_pos):
    x = x_ref[...]
    u = _rmsnorm(x, gpre_ref[...])
    ext_ref[pl.ds(pad, rows), :] = u

    step = lax.broadcasted_iota(jnp.int32, (rows, 1), 0) // stride
    pos = step + first_step + start_pos

    ys = []
    for g, w in enumerate(POOL_WINDOWS):
        cols = slice(g * POOL_GROUP, (g + 1) * POOL_GROUP)
        acc = ext_ref[:, cols]
        span = 1
        while span < w:
            acc = acc + pltpu.roll(acc, span * stride, axis=0)
            span *= 2
        win_sum = acc[pad:, :]
        count = jnp.minimum(pos + 1, w).astype(_F32)
        diff = win_sum / count - u[:, cols]
        ys.append(_dot(diff.astype(_BF16), wgrp_ref[g]))
    y = jnp.concatenate(ys, axis=-1) * scale_ref[...]
    o_ref[...] = x + _rmsnorm(y, gpost_ref[...])

    tail = ext_ref[pl.ds(rows, pad), :]
    newhist_ref[...] = tail.reshape(newhist_ref.shape)
    ext_ref[pl.ds(0, pad), :] = tail


def _mixer_kernel(x_ref, hist_ref, *refs, body, n_params, rows, blocks_per_seq,
                  n_prompt_blocks, pad_p, pad_s, stride_s):
    params = refs[:n_params]
    o_ref, newhist_p_ref, newhist_s_ref, ext_p_ref, ext_s_ref = refs[n_params:]
    i = pl.program_id(0)

    @pl.when(i < n_prompt_blocks)
    def _():
        j = lax.rem(i, jnp.int32(blocks_per_seq))

        @pl.when(j == 0)
        def _():
            ext_p_ref[pl.ds(0, pad_p), :] = jnp.zeros((pad_p, D_MODEL), _F32)

        body(x_ref, o_ref, newhist_p_ref, ext_p_ref, *params, rows=rows, stride=1, pad=pad_p,
             first_step=j * rows, start_pos=0)

    @pl.when(i == n_prompt_blocks)
    def _():
        ext_s_ref[pl.ds(0, pad_s), :] = hist_ref[...]
        body(x_ref, o_ref, newhist_s_ref, ext_s_ref, *params, rows=rows, stride=stride_s,
             pad=pad_s, first_step=0, start_pos=PAST_LEN)


def _mixer_sublayer(body, name, x_all, hist_s, h_idx, params, *, n_seq, seq_len, n_sample,
                    stride_s, pad_p, pad_s):
    rows = MIX_ROWS
    assert seq_len % rows == 0 and n_sample == rows and hist_s.shape[1] == pad_s
    blocks_per_seq = seq_len // rows
    npb = n_seq * blocks_per_seq
    row_spec = pl.BlockSpec((rows, D_MODEL), lambda i: (i, 0))
    kernel = functools.partial(
        _mixer_kernel, body=body, n_params=len(params), rows=rows, blocks_per_seq=blocks_per_seq,
        n_prompt_blocks=npb, pad_p=pad_p, pad_s=pad_s, stride_s=stride_s)
    return pl.pallas_call(
        kernel,
        grid=(npb + 1,),
        in_specs=[row_spec, _stacked(hist_s, h_idx)] + [_stacked(a, k) for a, k in params],
        out_specs=[row_spec,
                   pl.BlockSpec((1, pad_p, D_MODEL),
                                lambda i: (jnp.minimum(i, npb - 1) // blocks_per_seq, 0, 0)),
                   pl.BlockSpec((pad_s, D_MODEL), lambda i: (0, 0))],
        out_shape=[jax.ShapeDtypeStruct(x_all.shape, _F32),
                   jax.ShapeDtypeStruct((n_seq, pad_p, D_MODEL), _F32),
                   jax.ShapeDtypeStruct((pad_s, D_MODEL), _F32)],
        scratch_shapes=[pltpu.VMEM((pad_p + rows, D_MODEL), _F32),
                        pltpu.VMEM((pad_s + rows, D_MODEL), _F32)],
        input_output_aliases={0: 0},
        compiler_params=pltpu.CompilerParams(
            dimension_semantics=("arbitrary",), vmem_limit_bytes=VMEM_LIMIT_BYTES),
        name=name,
    )(x_all, hist_s, *[a for a, _ in params])


def kernel(x_prompt, x_sample, state_conv, state_pool, norm_gains, ffn_w_gate, ffn_w_up,
           ffn_w_down, conv_w_in, conv_kernel, conv_w_out, pool_w_group, pool_scale):
    batch, seq, _ = x_prompt.shape
    dec_batch, dec_seq, _ = x_sample.shape
    n_prompt = batch * seq
    n_sample = dec_batch * dec_seq
    n_conv = conv_w_in.shape[0]
    n_pool = pool_w_group.shape[0]

    xp_rows = x_prompt.reshape(n_prompt, D_MODEL)
    xs_rows = jnp.swapaxes(x_sample, 0, 1).reshape(n_sample, D_MODEL)

    gains = norm_gains.reshape(DEPTH * N_NORMS, 1, D_MODEL)
    ffn_w = (ffn_w_gate.reshape(2 * DEPTH, D_MODEL, D_FF),
             ffn_w_up.reshape(2 * DEPTH, D_MODEL, D_FF),
             ffn_w_down.reshape(2 * DEPTH, D_FF, D_MODEL))
    w_grp = pool_w_group.astype(_BF16)
    scale = pool_scale.reshape(n_pool, 1, D_MODEL)

    conv_pad_s = (CONV_W - 1) * dec_batch
    pool_pad_s = POOL_HIST * dec_batch
    conv_hist_s = jnp.swapaxes(state_conv, 1, 2).reshape(n_conv, conv_pad_s, D_MODEL)
    pool_hist_s = jnp.swapaxes(state_pool, 1, 2).reshape(n_pool, pool_pad_s, D_MODEL)
    ffn = functools.partial(_ffn_sublayer, n_prompt=n_prompt, n_sample=n_sample)
    w16 = [w[0].astype(_BF16) for w in ffn_w]
    ffn_chunk_rows = (CAST_ROWS_IN, CAST_ROWS_IN, CAST_ROWS_DOWN)

    def ffn_cast_jobs(idx):
        return [(w, idx, r) for w, r in zip(ffn_w, ffn_chunk_rows)]
    mixer = functools.partial(_mixer_sublayer, n_seq=batch, seq_len=seq, n_sample=n_sample,
                              stride_s=dec_batch)

    new_conv_p, new_conv_s, new_pool_p, new_pool_s = [], [], [], []
    for i in range(DEPTH):
        g0 = i * N_NORMS
        xs = (xp_rows, xs_rows) if i == 0 else (x_all,)
        j = i // 2
        jobs = ffn_cast_jobs(2 * i + 1)
        if i % 2 == 0:
            jobs += [(conv_w_in, j, CAST_ROWS_IN), (conv_w_out, j, CAST_ROWS_IN)]
        (x_all,), cast = ffn(xs, gains, g0, w16, jobs, mode="first" if i == 0 else "mid")
        w16, conv16 = cast[:3], cast[3:]
        norms = [(gains, g0 + 2), (gains, g0 + 3)]
        if i % 2 == 0:
            x_all, hp, hs = mixer(
                _conv_body, "conv_mixer", x_all, conv_hist_s, j,
                norms + [(conv16[0], None), (conv_kernel, j), (conv16[1], None)],
                pad_p=CONV_PAD_ROWS, pad_s=conv_pad_s)
            new_conv_p.append(hp[:, CONV_PAD_ROWS - (CONV_W - 1):, :])
            new_conv_s.append(jnp.swapaxes(hs.reshape(CONV_W - 1, dec_batch, D_MODEL), 0, 1))
        else:
            x_all, hp, hs = mixer(
                _pool_body, "pool_mixer", x_all, pool_hist_s, j,
                norms + [(w_grp, j), (scale, j)],
                pad_p=POOL_PAD_ROWS, pad_s=pool_pad_s)
            new_pool_p.append(hp[:, POOL_PAD_ROWS - POOL_HIST:, :])
            new_pool_s.append(jnp.swapaxes(hs.reshape(POOL_HIST, dec_batch, D_MODEL), 0, 1))
        if i == DEPTH - 1:
            (yp_rows, ys_rows), _ = ffn((x_all,), gains, g0 + 4, w16, [], mode="last")
        else:
            (x_all,), w16 = ffn((x_all,), gains, g0 + 4, w16, ffn_cast_jobs(2 * i + 2), mode="mid")

    y_prompt = yp_rows.reshape(batch, seq, D_MODEL)
    y_sample = jnp.swapaxes(ys_rows.reshape(dec_seq, dec_batch, D_MODEL), 0, 1)
    return (y_prompt, y_sample, jnp.stack(new_conv_p), jnp.stack(new_conv_s),
            jnp.stack(new_pool_p), jnp.stack(new_pool_s))
```

```python
import functools

import jax
import jax.numpy as jnp
from jax import lax
from jax.experimental import pallas as pl
from jax.experimental.pallas import tpu as pltpu

D_MODEL = 1024
D_FF = 2816
DEPTH = 4
N_NORMS = 6
CONV_W = 3
POOL_WINDOWS = (2, 4, 8, 16)
POOL_GROUP = D_MODEL // len(POOL_WINDOWS)
POOL_HIST = max(POOL_WINDOWS) - 1
PAST_LEN = 16384
EPS = 1e-6

SUBLANES = 8
FFN_ROWS = 512
FFN_SUB_ROWS = 256
CAST_ROWS_IN = 32
CAST_ROWS_DOWN = 176
MIX_ROWS = 512
CONV_PAD_ROWS = SUBLANES
POOL_PAD_ROWS = 2 * SUBLANES
VMEM_LIMIT_BYTES = 56 * 1024 * 1024
MIXER_VMEM_LIMIT_BYTES = 32 * 1024 * 1024
FFN_VMEM_LIMIT_BYTES = 24 * 1024 * 1024

_F32 = jnp.float32
_BF16 = jnp.bfloat16


def _rmsnorm(x, g):
    ms = jnp.mean(x * x, axis=-1, keepdims=True)
    return x * lax.rsqrt(ms + EPS) * g


def _dot(a, b):
    return jnp.dot(a, b, preferred_element_type=_F32)


def _whole(arr):
    nd = arr.ndim
    return pl.BlockSpec(arr.shape, lambda *_: (0,) * nd, pipeline_mode=pl.Buffered(1))


def _stacked(arr, idx):
    if idx is None:
        return _whole(arr)
    tail = arr.shape[1:]
    return pl.BlockSpec((None,) + tail, lambda *_: (idx,) + (0,) * len(tail),
                        pipeline_mode=pl.Buffered(1))


def _cast_side_job(i, last, jobs, dst_hbm, stage, ostage, sem_in, sem_out):
    slot = i & 1
    n_mats = len(jobs)
    chunk_rows = [st.shape[1] for st in stage]
    n_chunks = [dst.shape[0] // r for dst, r in zip(dst_hbm, chunk_rows)]
    by_count = {}
    for k, n in enumerate(n_chunks):
        by_count.setdefault(n, []).append(k)

    def rows_of(k, c):
        r = chunk_rows[k]
        return pl.ds(pl.multiple_of(c * r, r), r)

    def in_copy(k, c, s):
        src, idx = jobs[k]
        return pltpu.make_async_copy(src.at[idx, rows_of(k, c)], stage[k].at[s], sem_in.at[s, k])

    def out_copy(k, c, s):
        return pltpu.make_async_copy(ostage[k].at[s], dst_hbm[k].at[rows_of(k, c)],
                                     sem_out.at[s, k])

    def each(ks, fn):
        def run():
            for k in ks:
                fn(k)
        return run

    def prologue():
        pl.when(i == 0)(each(range(n_mats), lambda k: in_copy(k, 0, 0).start()))
        for n, ks in by_count.items():
            pl.when(i < n)(each(ks, lambda k: in_copy(k, i, slot).wait()))
            pl.when(i + 1 < n)(each(ks, lambda k: in_copy(k, i + 1, 1 - slot).start()))
            pl.when((i >= 2) & (i - 2 < n))(each(ks, lambda k: out_copy(k, i - 2, slot).wait()))

    def cast():
        for k in range(n_mats):
            ostage[k].at[slot][...] = stage[k].at[slot][...].astype(_BF16)

    def epilogue():
        for n, ks in by_count.items():
            pl.when(i < n)(each(ks, lambda k: out_copy(k, i, slot).start()))
            for c in range(max(last - 1, 0), n):
                assert c <= last
                pl.when(i == last)(each(ks, lambda k, c=c: out_copy(k, c, c & 1).wait()))

    return prologue, cast, epilogue


def _ffn_kernel(*refs, mode, n_prompt_blocks, n_steps, cast_idx):
    n_x = 2 if mode == "first" else 1
    n_out = 2 if mode == "last" else 1
    n_cast = len(cast_idx)
    x_refs, refs = refs[:n_x], refs[n_x:]
    (gpre_ref, gpost_ref, wg_ref, wu_ref, wd_ref), refs = refs[:5], refs[5:]
    src_hbm, refs = refs[:n_cast], refs[n_cast:]
    out_refs, refs = refs[:n_out], refs[n_out:]
    i = pl.program_id(0)
    is_prompt = i < n_prompt_blocks

    if n_cast:
        dst_hbm, stage, ostage = refs[:n_cast], refs[n_cast:2 * n_cast], refs[2 * n_cast:3 * n_cast]
        sem_in, sem_out = refs[3 * n_cast:]
        prologue, cast, epilogue = _cast_side_job(
            i, n_steps - 1, list(zip(src_hbm, cast_idx)), dst_hbm, stage, ostage, sem_in, sem_out)
        prologue()
        cast()

    for s in range(FFN_ROWS // FFN_SUB_ROWS):
        rows = pl.ds(s * FFN_SUB_ROWS, FFN_SUB_ROWS)
        if mode == "first":
            x = jnp.where(is_prompt, x_refs[0][rows, :], x_refs[1][rows, :])
        else:
            x = x_refs[0][rows, :]
        h = _rmsnorm(x, gpre_ref[...]).astype(_BF16)
        a = _dot(h, wg_ref[...])
        b = _dot(h, wu_ref[...])
        g = (a * jax.nn.sigmoid(a) * b).astype(_BF16)
        y = _dot(g, wd_ref[...])
        out = x + _rmsnorm(y, 0.5 * gpost_ref[...])
        if mode == "last":
            out_refs[1][rows, :] = out
        else:
            out_refs[0][rows, :] = out

    if mode == "last":
        @pl.when(is_prompt)
        def _():
            out_refs[0][...] = out_refs[1][...]

    if n_cast:
        epilogue()


def _ffn_sublayer(xs, gains, g_idx, w16, cast_jobs, *, mode, n_prompt, n_sample,
                  weights_in_vmem=False):
    n = n_prompt + n_sample
    rows = FFN_ROWS
    assert n_prompt % rows == 0 and n_sample == rows
    n_steps = n // rows
    npb = n_prompt // rows
    row_spec = pl.BlockSpec((rows, D_MODEL), lambda i: (i, 0))
    prompt_spec = pl.BlockSpec((rows, D_MODEL), lambda i: (jnp.minimum(i, npb - 1), 0))
    sample_spec = pl.BlockSpec((rows, D_MODEL), lambda i: (0, 0))
    any_spec = pl.BlockSpec(memory_space=pl.ANY)
    vmem_spec = pl.BlockSpec(memory_space=pltpu.VMEM)
    x_all_shape = jax.ShapeDtypeStruct((n, D_MODEL), _F32)
    if mode == "first":
        x_specs, out_specs, out_shape, aliases = [prompt_spec, sample_spec], [row_spec], [x_all_shape], {}
    elif mode == "last":
        x_specs, aliases = [row_spec], {}
        out_specs = [prompt_spec, sample_spec]
        out_shape = [jax.ShapeDtypeStruct((n_prompt, D_MODEL), _F32),
                     jax.ShapeDtypeStruct((n_sample, D_MODEL), _F32)]
    else:
        x_specs, out_specs, out_shape, aliases = [row_spec], [row_spec], [x_all_shape], {0: 0}
    n_main_out = len(out_specs)
    n_cast = len(cast_jobs)
    scratch = []
    if n_cast:
        for w, _, r in cast_jobs:
            assert w.shape[1] % r == 0 and w.shape[1] // r <= n_steps
        out_specs = out_specs + [any_spec] * n_cast
        out_shape = out_shape + [jax.ShapeDtypeStruct(w.shape[1:], _BF16) for w, _, _ in cast_jobs]
        scratch = ([pltpu.VMEM((2, r, w.shape[2]), _F32) for w, _, r in cast_jobs]
                   + [pltpu.VMEM((2, r, w.shape[2]), _BF16) for w, _, r in cast_jobs]
                   + [pltpu.SemaphoreType.DMA((2, n_cast)), pltpu.SemaphoreType.DMA((2, n_cast))])
    outs = pl.pallas_call(
        functools.partial(_ffn_kernel, mode=mode, n_prompt_blocks=npb, n_steps=n_steps,
                          cast_idx=tuple(idx for _, idx, _ in cast_jobs)),
        grid=(n_steps,),
        in_specs=x_specs + [_stacked(gains, g_idx), _stacked(gains, g_idx + 1)]
        + [vmem_spec if weights_in_vmem else _whole(w) for w in w16] + [any_spec] * n_cast,
        out_specs=out_specs,
        out_shape=out_shape,
        scratch_shapes=scratch,
        input_output_aliases=aliases,
        compiler_params=pltpu.CompilerParams(
            dimension_semantics=("arbitrary",),
            vmem_limit_bytes=FFN_VMEM_LIMIT_BYTES if weights_in_vmem else VMEM_LIMIT_BYTES),
        name="ffn_" + mode,
    )(*xs, gains, gains, *w16, *[w for w, _, _ in cast_jobs])
    return outs[:n_main_out], list(outs[n_main_out:])


def _conv_body(x_ref, o_ref, newhist_ref, ext_ref, gpre_ref, gpost_ref, win_ref, ck_ref,
               wout_ref, *, rows, stride, pad, first_step, start_pos):
    del first_step, start_pos
    x = x_ref[...]
    h = _rmsnorm(x, gpre_ref[...]).astype(_BF16)
    gate_c = _dot(h, win_ref[:, D_MODEL:2 * D_MODEL])
    v = _dot(h, win_ref[:, 2 * D_MODEL:])
    z = gate_c * v
    ext_ref[pl.ds(pad, rows), :] = z
    ck = ck_ref[...]
    conv = (ck[0:1] * ext_ref[pl.ds(pad - 2 * stride, rows), :]
            + ck[1:2] * ext_ref[pl.ds(pad - stride, rows), :]
            + ck[2:3] * z)
    gate_b = _dot(h, win_ref[:, :D_MODEL])
    y = _dot((gate_b * conv).astype(_BF16), wout_ref[...])
    o_ref[...] = x + _rmsnorm(y, gpost_ref[...])

    tail = ext_ref[pl.ds(rows, pad), :]
    newhist_ref[...] = tail.reshape(newhist_ref.shape)
    ext_ref[pl.ds(0, pad), :] = tail


def _pool_body(x_ref, o_ref, newhist_ref, ext_ref, gpre_ref, gpost_ref, wgrp_ref, scale_ref,
               *, rows, stride, pad, first_step, start_pos):
    x = x_ref[...]
    u = _rmsnorm(x, gpre_ref[...])
    ext_ref[pl.ds(pad, rows), :] = u

    step = lax.broadcasted_iota(jnp.int32, (rows, 1), 0) // stride
    pos = step + first_step + start_pos

    ys = []
    for g, w in enumerate(POOL_WINDOWS):
        cols = slice(g * POOL_GROUP, (g + 1) * POOL_GROUP)
        acc = ext_ref[:, cols]
        span = 1
        while span < w:
            acc = acc + pltpu.roll(acc, span * stride, axis=0)
            span *= 2
        win_sum = acc[pad:, :]
        count = jnp.minimum(pos + 1, w).astype(_F32)
        diff = win_sum / count - u[:, cols]
        ys.append(_dot(diff.astype(_BF16), wgrp_ref[g]))
    y = jnp.concatenate(ys, axis=-1) * scale_ref[...]
    o_ref[...] = x + _rmsnorm(y, gpost_ref[...])

    tail = ext_ref[pl.ds(rows, pad), :]
    newhist_ref[...] = tail.reshape(newhist_ref.shape)
    ext_ref[pl.ds(0, pad), :] = tail


def _mixer_kernel(x_ref, hist_ref, *refs, body, n_params, rows, blocks_per_seq,
                  n_prompt_blocks, pad_p, pad_s, stride_s):
    params = refs[:n_params]
    o_ref, newhist_p_ref, newhist_s_ref, ext_p_ref, ext_s_ref = refs[n_params:]
    i = pl.program_id(0)

    @pl.when(i < n_prompt_blocks)
    def _():
        j = lax.rem(i, jnp.int32(blocks_per_seq))

        @pl.when(j == 0)
        def _():
            ext_p_ref[pl.ds(0, pad_p), :] = jnp.zeros((pad_p, D_MODEL), _F32)

        body(x_ref, o_ref, newhist_p_ref, ext_p_ref, *params, rows=rows, stride=1, pad=pad_p,
             first_step=j * rows, start_pos=0)

    @pl.when(i == n_prompt_blocks)
    def _():
        ext_s_ref[pl.ds(0, pad_s), :] = hist_ref[...]
        body(x_ref, o_ref, newhist_s_ref, ext_s_ref, *params, rows=rows, stride=stride_s,
             pad=pad_s, first_step=0, start_pos=PAST_LEN)


def _mixer_sublayer(body, name, x_all, hist_s, h_idx, params, *, n_seq, seq_len, n_sample,
                    stride_s, pad_p, pad_s):
    rows = MIX_ROWS
    assert seq_len % rows == 0 and n_sample == rows and hist_s.shape[1] == pad_s
    blocks_per_seq = seq_len // rows
    npb = n_seq * blocks_per_seq
    row_spec = pl.BlockSpec((rows, D_MODEL), lambda i: (i, 0))
    kernel = functools.partial(
        _mixer_kernel, body=body, n_params=len(params), rows=rows, blocks_per_seq=blocks_per_seq,
        n_prompt_blocks=npb, pad_p=pad_p, pad_s=pad_s, stride_s=stride_s)
    return pl.pallas_call(
        kernel,
        grid=(npb + 1,),
        in_specs=[row_spec, _stacked(hist_s, h_idx)] + [_stacked(a, k) for a, k in params],
        out_specs=[row_spec,
                   pl.BlockSpec((1, pad_p, D_MODEL),
                                lambda i: (jnp.minimum(i, npb - 1) // blocks_per_seq, 0, 0)),
                   pl.BlockSpec((pad_s, D_MODEL), lambda i: (0, 0))],
        out_shape=[jax.ShapeDtypeStruct(x_all.shape, _F32),
                   jax.ShapeDtypeStruct((n_seq, pad_p, D_MODEL), _F32),
                   jax.ShapeDtypeStruct((pad_s, D_MODEL), _F32)],
        scratch_shapes=[pltpu.VMEM((pad_p + rows, D_MODEL), _F32),
                        pltpu.VMEM((pad_s + rows, D_MODEL), _F32)],
        input_output_aliases={0: 0},
        compiler_params=pltpu.CompilerParams(
            dimension_semantics=("arbitrary",), vmem_limit_bytes=MIXER_VMEM_LIMIT_BYTES),
        name=name,
    )(x_all, hist_s, *[a for a, _ in params])


def kernel(x_prompt, x_sample, state_conv, state_pool, norm_gains, ffn_w_gate, ffn_w_up,
           ffn_w_down, conv_w_in, conv_kernel, conv_w_out, pool_w_group, pool_scale):
    batch, seq, _ = x_prompt.shape
    dec_batch, dec_seq, _ = x_sample.shape
    n_prompt = batch * seq
    n_sample = dec_batch * dec_seq
    n_conv = conv_w_in.shape[0]
    n_pool = pool_w_group.shape[0]

    xp_rows = x_prompt.reshape(n_prompt, D_MODEL)
    xs_rows = jnp.swapaxes(x_sample, 0, 1).reshape(n_sample, D_MODEL)

    gains = norm_gains.reshape(DEPTH * N_NORMS, 1, D_MODEL)
    ffn_w = (ffn_w_gate.reshape(2 * DEPTH, D_MODEL, D_FF),
             ffn_w_up.reshape(2 * DEPTH, D_MODEL, D_FF),
             ffn_w_down.reshape(2 * DEPTH, D_FF, D_MODEL))
    w_grp = pool_w_group.astype(_BF16)
    scale = pool_scale.reshape(n_pool, 1, D_MODEL)

    conv_pad_s = (CONV_W - 1) * dec_batch
    pool_pad_s = POOL_HIST * dec_batch
    conv_hist_s = jnp.swapaxes(state_conv, 1, 2).reshape(n_conv, conv_pad_s, D_MODEL)
    pool_hist_s = jnp.swapaxes(state_pool, 1, 2).reshape(n_pool, pool_pad_s, D_MODEL)
    ffn = functools.partial(_ffn_sublayer, n_prompt=n_prompt, n_sample=n_sample)
    w16 = [w[0].astype(_BF16) for w in ffn_w]
    ffn_chunk_rows = (CAST_ROWS_IN, CAST_ROWS_IN, CAST_ROWS_DOWN)

    def ffn_cast_jobs(idx):
        return [(w, idx, r) for w, r in zip(ffn_w, ffn_chunk_rows)]
    mixer = functools.partial(_mixer_sublayer, n_seq=batch, seq_len=seq, n_sample=n_sample,
                              stride_s=dec_batch)

    new_conv_p, new_conv_s, new_pool_p, new_pool_s = [], [], [], []
    for i in range(DEPTH):
        g0 = i * N_NORMS
        xs = (xp_rows, xs_rows) if i == 0 else (x_all,)
        j = i // 2
        jobs = ffn_cast_jobs(2 * i + 1)
        if i % 2 == 0:
            jobs += [(conv_w_in, j, CAST_ROWS_IN), (conv_w_out, j, CAST_ROWS_IN)]
        (x_all,), cast = ffn(xs, gains, g0, w16, jobs, mode="first" if i == 0 else "mid")
        w16, conv16 = cast[:3], cast[3:]
        norms = [(gains, g0 + 2), (gains, g0 + 3)]
        if i % 2 == 0:
            x_all, hp, hs = mixer(
                _conv_body, "conv_mixer", x_all, conv_hist_s, j,
                norms + [(conv16[0], None), (conv_kernel, j), (conv16[1], None)],
                pad_p=CONV_PAD_ROWS, pad_s=conv_pad_s)
            new_conv_p.append(hp[:, CONV_PAD_ROWS - (CONV_W - 1):, :])
            new_conv_s.append(jnp.swapaxes(hs.reshape(CONV_W - 1, dec_batch, D_MODEL), 0, 1))
        else:
            x_all, hp, hs = mixer(
                _pool_body, "pool_mixer", x_all, pool_hist_s, j,
                norms + [(w_grp, j), (scale, j)],
                pad_p=POOL_PAD_ROWS, pad_s=pool_pad_s)
            new_pool_p.append(hp[:, POOL_PAD_ROWS - POOL_HIST:, :])
            new_pool_s.append(jnp.swapaxes(hs.reshape(POOL_HIST, dec_batch, D_MODEL), 0, 1))
        if i == DEPTH - 1:
            (yp_rows, ys_rows), _ = ffn((x_all,), gains, g0 + 4, w16, [], mode="last",
                                        weights_in_vmem=True)
        else:
            (x_all,), w16 = ffn((x_all,), gains, g0 + 4, w16, ffn_cast_jobs(2 * i + 2), mode="mid",
                                weights_in_vmem=True)

    y_prompt = yp_rows.reshape(batch, seq, D_MODEL)
    y_sample = jnp.swapaxes(ys_rows.reshape(dec_seq, dec_batch, D_MODEL), 0, 1)
    return (y_prompt, y_sample, jnp.stack(new_conv_p), jnp.stack(new_conv_s),
            jnp.stack(new_pool_p), jnp.stack(new_pool_s))
```

```python
import functools

import jax
import jax.numpy as jnp
from jax import lax
from jax.experimental import pallas as pl
from jax.experimental.pallas import tpu as pltpu

D_MODEL = 1024
D_FF = 2816
DEPTH = 4
N_NORMS = 6
CONV_W = 3
POOL_WINDOWS = (2, 4, 8, 16)
POOL_GROUP = D_MODEL // len(POOL_WINDOWS)
POOL_HIST = max(POOL_WINDOWS) - 1
PAST_LEN = 16384
EPS = 1e-6

SUBLANES = 8
FFN_ROWS = 512
FFN_SUB_ROWS = 256
CAST_ROWS_IN = 32
CAST_ROWS_DOWN = 176
MIX_ROWS = 512
CONV_PAD_ROWS = SUBLANES
POOL_PAD_ROWS = 2 * SUBLANES
FFN_VMEM_LIMIT_BYTES = 40 * 1024 * 1024
MIXER_VMEM_LIMIT_BYTES = 32 * 1024 * 1024

_F32 = jnp.float32
_BF16 = jnp.bfloat16


def _rmsnorm(x, g):
    ms = jnp.mean(x * x, axis=-1, keepdims=True)
    return x * lax.rsqrt(ms + EPS) * g


def _dot(a, b):
    return jnp.dot(a, b, preferred_element_type=_F32)


def _whole(arr):
    nd = arr.ndim
    return pl.BlockSpec(arr.shape, lambda *_: (0,) * nd, pipeline_mode=pl.Buffered(1))


def _stacked(arr, idx):
    if idx is None:
        return _whole(arr)
    tail = arr.shape[1:]
    return pl.BlockSpec((None,) + tail, lambda *_: (idx,) + (0,) * len(tail),
                        pipeline_mode=pl.Buffered(1))


def _cast_side_job(i, last, jobs, dst_hbm, stage, ostage, sem_in, sem_out):
    slot = i & 1
    n_mats = len(jobs)
    chunk_rows = [st.shape[1] for st in stage]
    n_chunks = [dst.shape[0] // r for dst, r in zip(dst_hbm, chunk_rows)]
    by_count = {}
    for k, n in enumerate(n_chunks):
        by_count.setdefault(n, []).append(k)

    def rows_of(k, c):
        r = chunk_rows[k]
        return pl.ds(pl.multiple_of(c * r, r), r)

    def in_copy(k, c, s):
        src, idx = jobs[k]
        return pltpu.make_async_copy(src.at[idx, rows_of(k, c)], stage[k].at[s], sem_in.at[s, k])

    def out_copy(k, c, s):
        return pltpu.make_async_copy(ostage[k].at[s], dst_hbm[k].at[rows_of(k, c)],
                                     sem_out.at[s, k])

    def each(ks, fn):
        def run():
            for k in ks:
                fn(k)
        return run

    def prologue():
        pl.when(i == 0)(each(range(n_mats), lambda k: in_copy(k, 0, 0).start()))
        for n, ks in by_count.items():
            pl.when(i < n)(each(ks, lambda k: in_copy(k, i, slot).wait()))
            pl.when(i + 1 < n)(each(ks, lambda k: in_copy(k, i + 1, 1 - slot).start()))
            pl.when((i >= 2) & (i - 2 < n))(each(ks, lambda k: out_copy(k, i - 2, slot).wait()))

    def cast():
        for k in range(n_mats):
            ostage[k].at[slot][...] = stage[k].at[slot][...].astype(_BF16)

    def epilogue():
        for n, ks in by_count.items():
            pl.when(i < n)(each(ks, lambda k: out_copy(k, i, slot).start()))
            for c in range(max(last - 1, 0), n):
                assert c <= last
                pl.when(i == last)(each(ks, lambda k, c=c: out_copy(k, c, c & 1).wait()))

    return prologue, cast, epilogue


def _ffn_kernel(*refs, mode, n_prompt_blocks, n_steps, cast_idx):
    n_x = 2 if mode == "first" else 1
    n_out = 2 if mode == "last" else 1
    n_cast = len(cast_idx)
    x_refs, refs = refs[:n_x], refs[n_x:]
    (gpre_ref, gpost_ref, wg_ref, wu_ref, wd_ref), refs = refs[:5], refs[5:]
    src_hbm, refs = refs[:n_cast], refs[n_cast:]
    out_refs, refs = refs[:n_out], refs[n_out:]
    i = pl.program_id(0)
    is_prompt = i < n_prompt_blocks

    if n_cast:
        dst_hbm, stage, ostage = refs[:n_cast], refs[n_cast:2 * n_cast], refs[2 * n_cast:3 * n_cast]
        sem_in, sem_out = refs[3 * n_cast:]
        prologue, cast, epilogue = _cast_side_job(
            i, n_steps - 1, list(zip(src_hbm, cast_idx)), dst_hbm, stage, ostage, sem_in, sem_out)
        prologue()
        cast()

    for s in range(FFN_ROWS // FFN_SUB_ROWS):
        rows = pl.ds(s * FFN_SUB_ROWS, FFN_SUB_ROWS)
        if mode == "first":
            x = jnp.where(is_prompt, x_refs[0][rows, :], x_refs[1][rows, :])
        else:
            x = x_refs[0][rows, :]
        h = _rmsnorm(x, gpre_ref[...]).astype(_BF16)
        a = _dot(h, wg_ref[...])
        b = _dot(h, wu_ref[...])
        g = (a * jax.nn.sigmoid(a) * b).astype(_BF16)
        y = _dot(g, wd_ref[...])
        out = x + _rmsnorm(y, 0.5 * gpost_ref[...])
        if mode == "last":
            out_refs[1][rows, :] = out
        else:
            out_refs[0][rows, :] = out

    if mode == "last":
        @pl.when(is_prompt)
        def _():
            out_refs[0][...] = out_refs[1][...]

    if n_cast:
        epilogue()


def _ffn_sublayer(xs, gains, g_idx, w16, cast_jobs, *, mode, n_prompt, n_sample):
    n = n_prompt + n_sample
    rows = FFN_ROWS
    assert n_prompt % rows == 0 and n_sample == rows
    n_steps = n // rows
    npb = n_prompt // rows
    row_spec = pl.BlockSpec((rows, D_MODEL), lambda i: (i, 0))
    prompt_spec = pl.BlockSpec((rows, D_MODEL), lambda i: (jnp.minimum(i, npb - 1), 0))
    sample_spec = pl.BlockSpec((rows, D_MODEL), lambda i: (0, 0))
    any_spec = pl.BlockSpec(memory_space=pl.ANY)
    x_all_shape = jax.ShapeDtypeStruct((n, D_MODEL), _F32)
    if mode == "first":
        x_specs, out_specs, out_shape, aliases = [prompt_spec, sample_spec], [row_spec], [x_all_shape], {}
    elif mode == "last":
        x_specs, aliases = [row_spec], {}
        out_specs = [prompt_spec, sample_spec]
        out_shape = [jax.ShapeDtypeStruct((n_prompt, D_MODEL), _F32),
                     jax.ShapeDtypeStruct((n_sample, D_MODEL), _F32)]
    else:
        x_specs, out_specs, out_shape, aliases = [row_spec], [row_spec], [x_all_shape], {0: 0}
    n_main_out = len(out_specs)
    n_cast = len(cast_jobs)
    scratch = []
    if n_cast:
        for w, _, r in cast_jobs:
            assert w.shape[1] % r == 0 and w.shape[1] // r <= n_steps
        out_specs = out_specs + [any_spec] * n_cast
        out_shape = out_shape + [jax.ShapeDtypeStruct(w.shape[1:], _BF16) for w, _, _ in cast_jobs]
        scratch = ([pltpu.VMEM((2, r, w.shape[2]), _F32) for w, _, r in cast_jobs]
                   + [pltpu.VMEM((2, r, w.shape[2]), _BF16) for w, _, r in cast_jobs]
                   + [pltpu.SemaphoreType.DMA((2, n_cast)), pltpu.SemaphoreType.DMA((2, n_cast))])
    outs = pl.pallas_call(
        functools.partial(_ffn_kernel, mode=mode, n_prompt_blocks=npb, n_steps=n_steps,
                          cast_idx=tuple(idx for _, idx, _ in cast_jobs)),
        grid=(n_steps,),
        in_specs=x_specs + [_stacked(gains, g_idx), _stacked(gains, g_idx + 1)]
        + [_whole(w) for w in w16] + [any_spec] * n_cast,
        out_specs=out_specs,
        out_shape=out_shape,
        scratch_shapes=scratch,
        input_output_aliases=aliases,
        compiler_params=pltpu.CompilerParams(
            dimension_semantics=("arbitrary",), vmem_limit_bytes=FFN_VMEM_LIMIT_BYTES),
        name="ffn_" + mode,
    )(*xs, gains, gains, *w16, *[w for w, _, _ in cast_jobs])
    return outs[:n_main_out], list(outs[n_main_out:])


def _conv_body(x_ref, o_ref, newhist_ref, ext_ref, gpre_ref, gpost_ref, win_ref, ck_ref,
               wout_ref, *, rows, stride, pad, first_step, start_pos):
    del first_step, start_pos
    x = x_ref[...]
    h = _rmsnorm(x, gpre_ref[...]).astype(_BF16)
    gate_c = _dot(h, win_ref[:, D_MODEL:2 * D_MODEL])
    v = _dot(h, win_ref[:, 2 * D_MODEL:])
    z = gate_c * v
    ext_ref[pl.ds(pad, rows), :] = z
    ck = ck_ref[...]
    conv = (ck[0:1] * ext_ref[pl.ds(pad - 2 * stride, rows), :]
            + ck[1:2] * ext_ref[pl.ds(pad - stride, rows), :]
            + ck[2:3] * z)
    gate_b = _dot(h, win_ref[:, :D_MODEL])
    y = _dot((gate_b * conv).astype(_BF16), wout_ref[...])
    o_ref[...] = x + _rmsnorm(y, gpost_ref[...])

    tail = ext_ref[pl.ds(rows, pad), :]
    newhist_ref[...] = tail.reshape(newhist_ref.shape)
    ext_ref[pl.ds(0, pad), :] = tail


def _pool_body(x_ref, o_ref, newhist_ref, ext_ref, gpre_ref, gpost_ref, wgrp_ref, scale_ref,
               *, rows, stride, pad, first_step, start_pos):
    x = x_ref[...]
    u = _rmsnorm(x, gpre_ref[...])
    ext_ref[pl.ds(pad, rows), :] = u

    step = lax.broadcasted_iota(jnp.int32, (rows, 1), 0) // stride
    pos = step + first_step + start_pos

    ys = []
    for g, w in enumerate(POOL_WINDOWS):
        cols = slice(g * POOL_GROUP, (g + 1) * POOL_GROUP)
        acc = ext_ref[:, cols]
        span = 1
        while span < w:
            acc = acc + pltpu.roll(acc, span * stride, axis=0)
            span *= 2
        win_sum = acc[pad:, :]
        count = jnp.minimum(pos + 1, w).astype(_F32)
        diff = win_sum / count - u[:, cols]
        ys.append(_dot(diff.astype(_BF16), wgrp_ref[g]))
    y = jnp.concatenate(ys, axis=-1) * scale_ref[...]
    o_ref[...] = x + _rmsnorm(y, gpost_ref[...])

    tail = ext_ref[pl.ds(rows, pad), :]
    newhist_ref[...] = tail.reshape(newhist_ref.shape)
    ext_ref[pl.ds(0, pad), :] = tail


def _mixer_kernel(x_ref, hist_ref, *refs, body, n_params, rows, blocks_per_seq,
                  n_prompt_blocks, pad_p, pad_s, stride_s):
    params = refs[:n_params]
    o_ref, newhist_p_ref, newhist_s_ref, ext_p_ref, ext_s_ref = refs[n_params:]
    i = pl.program_id(0)

    @pl.when(i < n_prompt_blocks)
    def _():
        j = lax.rem(i, jnp.int32(blocks_per_seq))

        @pl.when(j == 0)
        def _():
            ext_p_ref[pl.ds(0, pad_p), :] = jnp.zeros((pad_p, D_MODEL), _F32)

        body(x_ref, o_ref, newhist_p_ref, ext_p_ref, *params, rows=rows, stride=1, pad=pad_p,
             first_step=j * rows, start_pos=0)

    @pl.when(i == n_prompt_blocks)
    def _():
        ext_s_ref[pl.ds(0, pad_s), :] = hist_ref[...]
        body(x_ref, o_ref, newhist_s_ref, ext_s_ref, *params, rows=rows, stride=stride_s,
             pad=pad_s, first_step=0, start_pos=PAST_LEN)


def _mixer_sublayer(body, name, x_all, hist_s, h_idx, params, *, n_seq, seq_len, n_sample,
                    stride_s, pad_p, pad_s):
    rows = MIX_ROWS
    assert seq_len % rows == 0 and n_sample == rows and hist_s.shape[1] == pad_s
    blocks_per_seq = seq_len // rows
    npb = n_seq * blocks_per_seq
    row_spec = pl.BlockSpec((rows, D_MODEL), lambda i: (i, 0))
    kernel = functools.partial(
        _mixer_kernel, body=body, n_params=len(params), rows=rows, blocks_per_seq=blocks_per_seq,
        n_prompt_blocks=npb, pad_p=pad_p, pad_s=pad_s, stride_s=stride_s)
    return pl.pallas_call(
        kernel,
        grid=(npb + 1,),
        in_specs=[row_spec, _stacked(hist_s, h_idx)] + [_stacked(a, k) for a, k in params],
        out_specs=[row_spec,
                   pl.BlockSpec((1, pad_p, D_MODEL),
                                lambda i: (jnp.minimum(i, npb - 1) // blocks_per_seq, 0, 0)),
                   pl.BlockSpec((pad_s, D_MODEL), lambda i: (0, 0))],
        out_shape=[jax.ShapeDtypeStruct(x_all.shape, _F32),
                   jax.ShapeDtypeStruct((n_seq, pad_p, D_MODEL), _F32),
                   jax.ShapeDtypeStruct((pad_s, D_MODEL), _F32)],
        scratch_shapes=[pltpu.VMEM((pad_p + rows, D_MODEL), _F32),
                        pltpu.VMEM((pad_s + rows, D_MODEL), _F32)],
        input_output_aliases={0: 0},
        compiler_params=pltpu.CompilerParams(
            dimension_semantics=("arbitrary",), vmem_limit_bytes=MIXER_VMEM_LIMIT_BYTES),
        name=name,
    )(x_all, hist_s, *[a for a, _ in params])


def kernel(x_prompt, x_sample, state_conv, state_pool, norm_gains, ffn_w_gate, ffn_w_up,
           ffn_w_down, conv_w_in, conv_kernel, conv_w_out, pool_w_group, pool_scale):
    batch, seq, _ = x_prompt.shape
    dec_batch, dec_seq, _ = x_sample.shape
    n_prompt = batch * seq
    n_sample = dec_batch * dec_seq
    n_conv = conv_w_in.shape[0]
    n_pool = pool_w_group.shape[0]

    xp_rows = x_prompt.reshape(n_prompt, D_MODEL)
    xs_rows = jnp.swapaxes(x_sample, 0, 1).reshape(n_sample, D_MODEL)

    gains = norm_gains.reshape(DEPTH * N_NORMS, 1, D_MODEL)
    ffn_w = (ffn_w_gate.reshape(2 * DEPTH, D_MODEL, D_FF),
             ffn_w_up.reshape(2 * DEPTH, D_MODEL, D_FF),
             ffn_w_down.reshape(2 * DEPTH, D_FF, D_MODEL))
    w_grp = pool_w_group.astype(_BF16)
    scale = pool_scale.reshape(n_pool, 1, D_MODEL)

    conv_pad_s = (CONV_W - 1) * dec_batch
    pool_pad_s = POOL_HIST * dec_batch
    conv_hist_s = jnp.swapaxes(state_conv, 1, 2).reshape(n_conv, conv_pad_s, D_MODEL)
    pool_hist_s = jnp.swapaxes(state_pool, 1, 2).reshape(n_pool, pool_pad_s, D_MODEL)
    ffn = functools.partial(_ffn_sublayer, n_prompt=n_prompt, n_sample=n_sample)
    w16 = [w[0].astype(_BF16) for w in ffn_w]
    ffn_chunk_rows = (CAST_ROWS_IN, CAST_ROWS_IN, CAST_ROWS_DOWN)

    def ffn_cast_jobs(idx):
        return [(w, idx, r) for w, r in zip(ffn_w, ffn_chunk_rows)]
    mixer = functools.partial(_mixer_sublayer, n_seq=batch, seq_len=seq, n_sample=n_sample,
                              stride_s=dec_batch)

    new_conv_p, new_conv_s, new_pool_p, new_pool_s = [], [], [], []
    for i in range(DEPTH):
        g0 = i * N_NORMS
        xs = (xp_rows, xs_rows) if i == 0 else (x_all,)
        j = i // 2
        jobs = ffn_cast_jobs(2 * i + 1)
        if i % 2 == 0:
            jobs += [(conv_w_in, j, CAST_ROWS_IN), (conv_w_out, j, CAST_ROWS_IN)]
        (x_all,), cast = ffn(xs, gains, g0, w16, jobs, mode="first" if i == 0 else "mid")
        w16, conv16 = cast[:3], cast[3:]
        norms = [(gains, g0 + 2), (gains, g0 + 3)]
        if i % 2 == 0:
            x_all, hp, hs = mixer(
                _conv_body, "conv_mixer", x_all, conv_hist_s, j,
                norms + [(conv16[0], None), (conv_kernel, j), (conv16[1], None)],
                pad_p=CONV_PAD_ROWS, pad_s=conv_pad_s)
            new_conv_p.append(hp[:, CONV_PAD_ROWS - (CONV_W - 1):, :])
            new_conv_s.append(jnp.swapaxes(hs.reshape(CONV_W - 1, dec_batch, D_MODEL), 0, 1))
        else:
            x_all, hp, hs = mixer(
                _pool_body, "pool_mixer", x_all, pool_hist_s, j,
                norms + [(w_grp, j), (scale, j)],
                pad_p=POOL_PAD_ROWS, pad_s=pool_pad_s)
            new_pool_p.append(hp[:, POOL_PAD_ROWS - POOL_HIST:, :])
            new_pool_s.append(jnp.swapaxes(hs.reshape(POOL_HIST, dec_batch, D_MODEL), 0, 1))
        if i == DEPTH - 1:
            (yp_rows, ys_rows), _ = ffn((x_all,), gains, g0 + 4, w16, [], mode="last")
        else:
            (x_all,), w16 = ffn((x_all,), gains, g0 + 4, w16, ffn_cast_jobs(2 * i + 2), mode="mid")

    y_prompt = yp_rows.reshape(batch, seq, D_MODEL)
    y_sample = jnp.swapaxes(ys_rows.reshape(dec_seq, dec_batch, D_MODEL), 0, 1)
    return (y_prompt, y_sample, jnp.stack(new_conv_p), jnp.stack(new_conv_s),
            jnp.stack(new_pool_p), jnp.stack(new_pool_s))
```

```python
import functools

import jax
import jax.numpy as jnp
from jax import lax
from jax.experimental import pallas as pl
from jax.experimental.pallas import tpu as pltpu

D_MODEL = 1024
D_FF = 2816
DEPTH = 4
N_NORMS = 6
CONV_W = 3
POOL_WINDOWS = (2, 4, 8, 16)
POOL_GROUP = D_MODEL // len(POOL_WINDOWS)
POOL_HIST = max(POOL_WINDOWS) - 1
PAST_LEN = 16384
EPS = 1e-6

SUBLANES = 8
FFN_ROWS = 512
FFN_SUB_ROWS = 256
CAST_ROWS_IN = 32
CAST_ROWS_DOWN = 176
MIX_ROWS = 512
CONV_PAD_ROWS = SUBLANES
POOL_PAD_ROWS = 2 * SUBLANES
FFN_VMEM_LIMIT_BYTES = 40 * 1024 * 1024
FFN_CAST_VMEM_LIMIT_BYTES = 56 * 1024 * 1024
MIXER_VMEM_LIMIT_BYTES = 32 * 1024 * 1024

_F32 = jnp.float32
_BF16 = jnp.bfloat16


def _rmsnorm(x, g):
    ms = jnp.mean(x * x, axis=-1, keepdims=True)
    return x * lax.rsqrt(ms + EPS) * g


def _dot(a, b):
    return jnp.dot(a, b, preferred_element_type=_F32)


def _whole(arr):
    nd = arr.ndim
    return pl.BlockSpec(arr.shape, lambda *_: (0,) * nd, pipeline_mode=pl.Buffered(1))


def _stacked(arr, idx):
    if idx is None:
        return _whole(arr)
    tail = arr.shape[1:]
    return pl.BlockSpec((None,) + tail, lambda *_: (idx,) + (0,) * len(tail),
                        pipeline_mode=pl.Buffered(1))


def _cast_side_job(i, last, jobs, dst_hbm, stage, ostage, sem_in, sem_out):
    slot = i & 1
    n_mats = len(jobs)
    chunk_rows = [st.shape[1] for st in stage]
    n_chunks = [dst.shape[0] // r for dst, r in zip(dst_hbm, chunk_rows)]
    by_count = {}
    for k, n in enumerate(n_chunks):
        by_count.setdefault(n, []).append(k)

    def rows_of(k, c):
        r = chunk_rows[k]
        return pl.ds(pl.multiple_of(c * r, r), r)

    def in_copy(k, c, s):
        src, idx = jobs[k]
        return pltpu.make_async_copy(src.at[idx, rows_of(k, c)], stage[k].at[s], sem_in.at[s, k])

    def out_copy(k, c, s):
        return pltpu.make_async_copy(ostage[k].at[s], dst_hbm[k].at[rows_of(k, c)],
                                     sem_out.at[s, k])

    def each(ks, fn):
        def run():
            for k in ks:
                fn(k)
        return run

    def prologue():
        pl.when(i == 0)(each(range(n_mats), lambda k: in_copy(k, 0, 0).start()))
        for n, ks in by_count.items():
            pl.when(i < n)(each(ks, lambda k: in_copy(k, i, slot).wait()))
            pl.when(i + 1 < n)(each(ks, lambda k: in_copy(k, i + 1, 1 - slot).start()))
            pl.when((i >= 2) & (i - 2 < n))(each(ks, lambda k: out_copy(k, i - 2, slot).wait()))

    def cast():
        for k in range(n_mats):
            ostage[k].at[slot][...] = stage[k].at[slot][...].astype(_BF16)

    def epilogue():
        for n, ks in by_count.items():
            pl.when(i < n)(each(ks, lambda k: out_copy(k, i, slot).start()))
            for c in range(max(last - 1, 0), n):
                assert c <= last
                pl.when(i == last)(each(ks, lambda k, c=c: out_copy(k, c, c & 1).wait()))

    return prologue, cast, epilogue


def _ffn_kernel(*refs, mode, n_prompt_blocks, n_steps, cast_idx):
    n_x = 2 if mode == "first" else 1
    n_out = 2 if mode == "last" else 1
    n_cast = len(cast_idx)
    x_refs, refs = refs[:n_x], refs[n_x:]
    (gpre_ref, gpost_ref, wg_ref, wu_ref, wd_ref), refs = refs[:5], refs[5:]
    src_hbm, refs = refs[:n_cast], refs[n_cast:]
    out_refs, refs = refs[:n_out], refs[n_out:]
    i = pl.program_id(0)
    is_prompt = i < n_prompt_blocks

    if n_cast:
        dst_hbm, stage, ostage = refs[:n_cast], refs[n_cast:2 * n_cast], refs[2 * n_cast:3 * n_cast]
        sem_in, sem_out = refs[3 * n_cast:]
        prologue, cast, epilogue = _cast_side_job(
            i, n_steps - 1, list(zip(src_hbm, cast_idx)), dst_hbm, stage, ostage, sem_in, sem_out)
        prologue()
        cast()

    for s in range(FFN_ROWS // FFN_SUB_ROWS):
        rows = pl.ds(s * FFN_SUB_ROWS, FFN_SUB_ROWS)
        if mode == "first":
            x = jnp.where(is_prompt, x_refs[0][rows, :], x_refs[1][rows, :])
        else:
            x = x_refs[0][rows, :]
        h = _rmsnorm(x, gpre_ref[...]).astype(_BF16)
        a = _dot(h, wg_ref[...])
        b = _dot(h, wu_ref[...])
        g = (a * jax.nn.sigmoid(a) * b).astype(_BF16)
        y = _dot(g, wd_ref[...])
        out = x + _rmsnorm(y, 0.5 * gpost_ref[...])
        if mode == "last":
            out_refs[1][rows, :] = out
        else:
            out_refs[0][rows, :] = out

    if mode == "last":
        @pl.when(is_prompt)
        def _():
            out_refs[0][...] = out_refs[1][...]

    if n_cast:
        epilogue()


def _ffn_sublayer(xs, gains, g_idx, w16, cast_jobs, *, mode, n_prompt, n_sample):
    n = n_prompt + n_sample
    rows = FFN_ROWS
    assert n_prompt % rows == 0 and n_sample == rows
    n_steps = n // rows
    npb = n_prompt // rows
    row_spec = pl.BlockSpec((rows, D_MODEL), lambda i: (i, 0))
    prompt_spec = pl.BlockSpec((rows, D_MODEL), lambda i: (jnp.minimum(i, npb - 1), 0))
    sample_spec = pl.BlockSpec((rows, D_MODEL), lambda i: (0, 0))
    any_spec = pl.BlockSpec(memory_space=pl.ANY)
    x_all_shape = jax.ShapeDtypeStruct((n, D_MODEL), _F32)
    if mode == "first":
        x_specs, out_specs, out_shape, aliases = [prompt_spec, sample_spec], [row_spec], [x_all_shape], {}
    elif mode == "last":
        x_specs, aliases = [row_spec], {}
        out_specs = [prompt_spec, sample_spec]
        out_shape = [jax.ShapeDtypeStruct((n_prompt, D_MODEL), _F32),
                     jax.ShapeDtypeStruct((n_sample, D_MODEL), _F32)]
    else:
        x_specs, out_specs, out_shape, aliases = [row_spec], [row_spec], [x_all_shape], {0: 0}
    n_main_out = len(out_specs)
    n_cast = len(cast_jobs)
    scratch = []
    if n_cast:
        for w, _, r in cast_jobs:
            assert w.shape[1] % r == 0 and w.shape[1] // r <= n_steps
        out_specs = out_specs + [any_spec] * n_cast
        out_shape = out_shape + [jax.ShapeDtypeStruct(w.shape[1:], _BF16) for w, _, _ in cast_jobs]
        scratch = ([pltpu.VMEM((2, r, w.shape[2]), _F32) for w, _, r in cast_jobs]
                   + [pltpu.VMEM((2, r, w.shape[2]), _BF16) for w, _, r in cast_jobs]
                   + [pltpu.SemaphoreType.DMA((2, n_cast)), pltpu.SemaphoreType.DMA((2, n_cast))])
    outs = pl.pallas_call(
        functools.partial(_ffn_kernel, mode=mode, n_prompt_blocks=npb, n_steps=n_steps,
                          cast_idx=tuple(idx for _, idx, _ in cast_jobs)),
        grid=(n_steps,),
        in_specs=x_specs + [_stacked(gains, g_idx), _stacked(gains, g_idx + 1)]
        + [_whole(w) for w in w16] + [any_spec] * n_cast,
        out_specs=out_specs,
        out_shape=out_shape,
        scratch_shapes=scratch,
        input_output_aliases=aliases,
        compiler_params=pltpu.CompilerParams(
            dimension_semantics=("arbitrary",),
            vmem_limit_bytes=FFN_CAST_VMEM_LIMIT_BYTES if n_cast else FFN_VMEM_LIMIT_BYTES),
        name="ffn_" + mode,
    )(*xs, gains, gains, *w16, *[w for w, _, _ in cast_jobs])
    return outs[:n_main_out], list(outs[n_main_out:])


def _conv_body(x_ref, o_ref, newhist_ref, ext_ref, gpre_ref, gpost_ref, win_ref, ck_ref,
               wout_ref, *, rows, stride, pad, first_step, start_pos):
    del first_step, start_pos
    x = x_ref[...]
    h = _rmsnorm(x, gpre_ref[...]).astype(_BF16)
    gate_c = _dot(h, win_ref[:, D_MODEL:2 * D_MODEL])
    v = _dot(h, win_ref[:, 2 * D_MODEL:])
    z = gate_c * v
    ext_ref[pl.ds(pad, rows), :] = z
    ck = ck_ref[...]
    conv = (ck[0:1] * ext_ref[pl.ds(pad - 2 * stride, rows), :]
            + ck[1:2] * ext_ref[pl.ds(pad - stride, rows), :]
            + ck[2:3] * z)
    gate_b = _dot(h, win_ref[:, :D_MODEL])
    y = _dot((gate_b * conv).astype(_BF16), wout_ref[...])
    o_ref[...] = x + _rmsnorm(y, gpost_ref[...])

    tail = ext_ref[pl.ds(rows, pad), :]
    newhist_ref[...] = tail.reshape(newhist_ref.shape)
    ext_ref[pl.ds(0, pad), :] = tail


def _pool_body(x_ref, o_ref, newhist_ref, ext_ref, gpre_ref, gpost_ref, wgrp_ref, scale_ref,
               *, rows, stride, pad, first_step, start_pos):
    x = x_ref[...]
    u = _rmsnorm(x, gpre_ref[...])
    ext_ref[pl.ds(pad, rows), :] = u

    step = lax.broadcasted_iota(jnp.int32, (rows, 1), 0) // stride
    pos = step + first_step + start_pos

    ys = []
    for g, w in enumerate(POOL_WINDOWS):
        cols = slice(g * POOL_GROUP, (g + 1) * POOL_GROUP)
        acc = ext_ref[:, cols]
        span = 1
        while span < w:
            acc = acc + pltpu.roll(acc, span * stride, axis=0)
            span *= 2
        win_sum = acc[pad:, :]
        count = jnp.minimum(pos + 1, w).astype(_F32)
        diff = win_sum / count - u[:, cols]
        ys.append(_dot(diff.astype(_BF16), wgrp_ref[g]))
    y = jnp.concatenate(ys, axis=-1) * scale_ref[...]
    o_ref[...] = x + _rmsnorm(y, gpost_ref[...])

    tail = ext_ref[pl.ds(rows, pad), :]
    newhist_ref[...] = tail.reshape(newhist_ref.shape)
    ext_ref[pl.ds(0, pad), :] = tail


def _mixer_kernel(x_ref, hist_ref, *refs, body, n_params, rows, blocks_per_seq,
                  n_prompt_blocks, pad_p, pad_s, stride_s):
    params = refs[:n_params]
    o_ref, newhist_p_ref, newhist_s_ref, ext_p_ref, ext_s_ref = refs[n_params:]
    i = pl.program_id(0)

    @pl.when(i < n_prompt_blocks)
    def _():
        j = lax.rem(i, jnp.int32(blocks_per_seq))

        @pl.when(j == 0)
        def _():
            ext_p_ref[pl.ds(0, pad_p), :] = jnp.zeros((pad_p, D_MODEL), _F32)

        body(x_ref, o_ref, newhist_p_ref, ext_p_ref, *params, rows=rows, stride=1, pad=pad_p,
             first_step=j * rows, start_pos=0)

    @pl.when(i == n_prompt_blocks)
    def _():
        ext_s_ref[pl.ds(0, pad_s), :] = hist_ref[...]
        body(x_ref, o_ref, newhist_s_ref, ext_s_ref, *params, rows=rows, stride=stride_s,
             pad=pad_s, first_step=0, start_pos=PAST_LEN)


def _mixer_sublayer(body, name, x_all, hist_s, h_idx, params, *, n_seq, seq_len, n_sample,
                    stride_s, pad_p, pad_s):
    rows = MIX_ROWS
    assert seq_len % rows == 0 and n_sample == rows and hist_s.shape[1] == pad_s
    blocks_per_seq = seq_len // rows
    npb = n_seq * blocks_per_seq
    row_spec = pl.BlockSpec((rows, D_MODEL), lambda i: (i, 0))
    kernel = functools.partial(
        _mixer_kernel, body=body, n_params=len(params), rows=rows, blocks_per_seq=blocks_per_seq,
        n_prompt_blocks=npb, pad_p=pad_p, pad_s=pad_s, stride_s=stride_s)
    return pl.pallas_call(
        kernel,
        grid=(npb + 1,),
        in_specs=[row_spec, _stacked(hist_s, h_idx)] + [_stacked(a, k) for a, k in params],
        out_specs=[row_spec,
                   pl.BlockSpec((1, pad_p, D_MODEL),
                                lambda i: (jnp.minimum(i, npb - 1) // blocks_per_seq, 0, 0)),
                   pl.BlockSpec((pad_s, D_MODEL), lambda i: (0, 0))],
        out_shape=[jax.ShapeDtypeStruct(x_all.shape, _F32),
                   jax.ShapeDtypeStruct((n_seq, pad_p, D_MODEL), _F32),
                   jax.ShapeDtypeStruct((pad_s, D_MODEL), _F32)],
        scratch_shapes=[pltpu.VMEM((pad_p + rows, D_MODEL), _F32),
                        pltpu.VMEM((pad_s + rows, D_MODEL), _F32)],
        input_output_aliases={0: 0},
        compiler_params=pltpu.CompilerParams(
            dimension_semantics=("arbitrary",), vmem_limit_bytes=MIXER_VMEM_LIMIT_BYTES),
        name=name,
    )(x_all, hist_s, *[a for a, _ in params])


def kernel(x_prompt, x_sample, state_conv, state_pool, norm_gains, ffn_w_gate, ffn_w_up,
           ffn_w_down, conv_w_in, conv_kernel, conv_w_out, pool_w_group, pool_scale):
    batch, seq, _ = x_prompt.shape
    dec_batch, dec_seq, _ = x_sample.shape
    n_prompt = batch * seq
    n_sample = dec_batch * dec_seq
    n_conv = conv_w_in.shape[0]
    n_pool = pool_w_group.shape[0]

    xp_rows = x_prompt.reshape(n_prompt, D_MODEL)
    xs_rows = jnp.swapaxes(x_sample, 0, 1).reshape(n_sample, D_MODEL)

    gains = norm_gains.reshape(DEPTH * N_NORMS, 1, D_MODEL)
    ffn_w = (ffn_w_gate.reshape(2 * DEPTH, D_MODEL, D_FF),
             ffn_w_up.reshape(2 * DEPTH, D_MODEL, D_FF),
             ffn_w_down.reshape(2 * DEPTH, D_FF, D_MODEL))
    w_grp = pool_w_group.astype(_BF16)
    scale = pool_scale.reshape(n_pool, 1, D_MODEL)

    conv_pad_s = (CONV_W - 1) * dec_batch
    pool_pad_s = POOL_HIST * dec_batch
    conv_hist_s = jnp.swapaxes(state_conv, 1, 2).reshape(n_conv, conv_pad_s, D_MODEL)
    pool_hist_s = jnp.swapaxes(state_pool, 1, 2).reshape(n_pool, pool_pad_s, D_MODEL)
    ffn = functools.partial(_ffn_sublayer, n_prompt=n_prompt, n_sample=n_sample)
    w16 = [w[0].astype(_BF16) for w in ffn_w]
    ffn_chunk_rows = (CAST_ROWS_IN, CAST_ROWS_IN, CAST_ROWS_DOWN)

    def ffn_cast_jobs(idx):
        return [(w, idx, r) for w, r in zip(ffn_w, ffn_chunk_rows)]
    mixer = functools.partial(_mixer_sublayer, n_seq=batch, seq_len=seq, n_sample=n_sample,
                              stride_s=dec_batch)

    new_conv_p, new_conv_s, new_pool_p, new_pool_s = [], [], [], []
    for i in range(DEPTH):
        g0 = i * N_NORMS
        xs = (xp_rows, xs_rows) if i == 0 else (x_all,)
        j = i // 2
        jobs = ffn_cast_jobs(2 * i + 1)
        if i % 2 == 0:
            jobs += [(conv_w_in, j, CAST_ROWS_IN), (conv_w_out, j, CAST_ROWS_IN)]
        (x_all,), cast = ffn(xs, gains, g0, w16, jobs, mode="first" if i == 0 else "mid")
        w16, conv16 = cast[:3], cast[3:]
        norms = [(gains, g0 + 2), (gains, g0 + 3)]
        if i % 2 == 0:
            x_all, hp, hs = mixer(
                _conv_body, "conv_mixer", x_all, conv_hist_s, j,
                norms + [(conv16[0], None), (conv_kernel, j), (conv16[1], None)],
                pad_p=CONV_PAD_ROWS, pad_s=conv_pad_s)
            new_conv_p.append(hp[:, CONV_PAD_ROWS - (CONV_W - 1):, :])
            new_conv_s.append(jnp.swapaxes(hs.reshape(CONV_W - 1, dec_batch, D_MODEL), 0, 1))
        else:
            x_all, hp, hs = mixer(
                _pool_body, "pool_mixer", x_all, pool_hist_s, j,
                norms + [(w_grp, j), (scale, j)],
                pad_p=POOL_PAD_ROWS, pad_s=pool_pad_s)
            new_pool_p.append(hp[:, POOL_PAD_ROWS - POOL_HIST:, :])
            new_pool_s.append(jnp.swapaxes(hs.reshape(POOL_HIST, dec_batch, D_MODEL), 0, 1))
        if i == DEPTH - 1:
            (yp_rows, ys_rows), _ = ffn((x_all,), gains, g0 + 4, w16, [], mode="last")
        else:
            (x_all,), w16 = ffn((x_all,), gains, g0 + 4, w16, ffn_cast_jobs(2 * i + 2), mode="mid")

    y_prompt = yp_rows.reshape(batch, seq, D_MODEL)
    y_sample = jnp.swapaxes(ys_rows.reshape(dec_seq, dec_batch, D_MODEL), 0, 1)
    return (y_prompt, y_sample, jnp.stack(new_conv_p), jnp.stack(new_conv_s),
            jnp.stack(new_pool_p), jnp.stack(new_pool_s))
```

```python
import functools

import jax
import jax.numpy as jnp
from jax import lax
from jax.experimental import pallas as pl
from jax.experimental.pallas import tpu as pltpu

D_MODEL = 1024
D_FF = 2816
DEPTH = 4
N_NORMS = 6
CONV_W = 3
POOL_WINDOWS = (2, 4, 8, 16)
POOL_GROUP = D_MODEL // len(POOL_WINDOWS)
POOL_HIST = max(POOL_WINDOWS) - 1
PAST_LEN = 16384
EPS = 1e-6

SUBLANES = 8
FFN_ROWS = 512
FFN_SUB_ROWS = 256
CAST_ROWS_IN = 32
CAST_ROWS_DOWN = 176
MIX_ROWS = 512
CONV_PAD_ROWS = SUBLANES
POOL_PAD_ROWS = 2 * SUBLANES
FFN_VMEM_LIMIT_BYTES = 40 * 1024 * 1024
MIXER_VMEM_LIMIT_BYTES = 32 * 1024 * 1024

_F32 = jnp.float32
_BF16 = jnp.bfloat16


def _rmsnorm(x, g):
    ms = jnp.mean(x * x, axis=-1, keepdims=True)
    return x * lax.rsqrt(ms + EPS) * g


def _dot(a, b):
    return jnp.dot(a, b, preferred_element_type=_F32)


def _whole(arr):
    nd = arr.ndim
    return pl.BlockSpec(arr.shape, lambda *_: (0,) * nd, pipeline_mode=pl.Buffered(1))


def _stacked(arr, idx):
    if idx is None:
        return _whole(arr)
    tail = arr.shape[1:]
    return pl.BlockSpec((None,) + tail, lambda *_: (idx,) + (0,) * len(tail),
                        pipeline_mode=pl.Buffered(1))


def _cast_side_job(i, last, jobs, dst_hbm, stage, ostage, sem_in, sem_out):
    slot = i & 1
    n_mats = len(jobs)
    chunk_rows = [st.shape[1] for st in stage]
    n_chunks = [dst.shape[0] // r for dst, r in zip(dst_hbm, chunk_rows)]
    by_count = {}
    for k, n in enumerate(n_chunks):
        by_count.setdefault(n, []).append(k)

    def rows_of(k, c):
        r = chunk_rows[k]
        return pl.ds(pl.multiple_of(c * r, r), r)

    def in_copy(k, c, s):
        src, idx = jobs[k]
        return pltpu.make_async_copy(src.at[idx, rows_of(k, c)], stage[k].at[s], sem_in.at[s, k])

    def out_copy(k, c, s):
        return pltpu.make_async_copy(ostage[k].at[s], dst_hbm[k].at[rows_of(k, c)],
                                     sem_out.at[s, k])

    def each(ks, fn):
        def run():
            for k in ks:
                fn(k)
        return run

    def prologue():
        pl.when(i == 0)(each(range(n_mats), lambda k: in_copy(k, 0, 0).start()))
        for n, ks in by_count.items():
            pl.when(i < n)(each(ks, lambda k: in_copy(k, i, slot).wait()))
            pl.when(i + 1 < n)(each(ks, lambda k: in_copy(k, i + 1, 1 - slot).start()))
            pl.when((i >= 2) & (i - 2 < n))(each(ks, lambda k: out_copy(k, i - 2, slot).wait()))

    def cast():
        for k in range(n_mats):
            ostage[k].at[slot][...] = stage[k].at[slot][...].astype(_BF16)

    def epilogue():
        for n, ks in by_count.items():
            pl.when(i < n)(each(ks, lambda k: out_copy(k, i, slot).start()))
            for c in range(max(last - 1, 0), n):
                assert c <= last
                pl.when(i == last)(each(ks, lambda k, c=c: out_copy(k, c, c & 1).wait()))

    return prologue, cast, epilogue


def _ffn_kernel(*refs, mode, n_prompt_blocks, n_steps, cast_idx):
    n_x = 2 if mode == "first" else 1
    n_out = 2 if mode == "last" else 1
    n_cast = len(cast_idx)
    x_refs, refs = refs[:n_x], refs[n_x:]
    (gpre_ref, gpost_ref, wg_ref, wu_ref, wd_ref), refs = refs[:5], refs[5:]
    src_hbm, refs = refs[:n_cast], refs[n_cast:]
    out_refs, refs = refs[:n_out], refs[n_out:]
    i = pl.program_id(0)
    is_prompt = i < n_prompt_blocks

    if n_cast:
        dst_hbm, stage, ostage = refs[:n_cast], refs[n_cast:2 * n_cast], refs[2 * n_cast:3 * n_cast]
        sem_in, sem_out = refs[3 * n_cast:]
        prologue, cast, epilogue = _cast_side_job(
            i, n_steps - 1, list(zip(src_hbm, cast_idx)), dst_hbm, stage, ostage, sem_in, sem_out)
        prologue()
        cast()

    for s in range(FFN_ROWS // FFN_SUB_ROWS):
        rows = pl.ds(s * FFN_SUB_ROWS, FFN_SUB_ROWS)
        if mode == "first":
            x = jnp.where(is_prompt, x_refs[0][rows, :], x_refs[1][rows, :])
        else:
            x = x_refs[0][rows, :]
        h = _rmsnorm(x, gpre_ref[...]).astype(_BF16)
        a = _dot(h, wg_ref[...])
        b = _dot(h, wu_ref[...])
        g = (a * jax.nn.sigmoid(a) * b).astype(_BF16)
        y = _dot(g, wd_ref[...])
        out = x + _rmsnorm(y, 0.5 * gpost_ref[...])
        if mode == "last":
            out_refs[1][rows, :] = out
        else:
            out_refs[0][rows, :] = out

    if mode == "last":
        @pl.when(is_prompt)
        def _():
            out_refs[0][...] = out_refs[1][...]

    if n_cast:
        epilogue()


def _ffn_sublayer(xs, gains, g_idx, w16, cast_jobs, *, mode, n_prompt, n_sample):
    n = n_prompt + n_sample
    rows = FFN_ROWS
    assert n_prompt % rows == 0 and n_sample == rows
    n_steps = n // rows
    npb = n_prompt // rows
    row_spec = pl.BlockSpec((rows, D_MODEL), lambda i: (i, 0))
    prompt_spec = pl.BlockSpec((rows, D_MODEL), lambda i: (jnp.minimum(i, npb - 1), 0))
    sample_spec = pl.BlockSpec((rows, D_MODEL), lambda i: (0, 0))
    any_spec = pl.BlockSpec(memory_space=pltpu.HBM)
    x_all_shape = jax.ShapeDtypeStruct((n, D_MODEL), _F32)
    if mode == "first":
        x_specs, out_specs, out_shape, aliases = [prompt_spec, sample_spec], [row_spec], [x_all_shape], {}
    elif mode == "last":
        x_specs, aliases = [row_spec], {}
        out_specs = [prompt_spec, sample_spec]
        out_shape = [jax.ShapeDtypeStruct((n_prompt, D_MODEL), _F32),
                     jax.ShapeDtypeStruct((n_sample, D_MODEL), _F32)]
    else:
        x_specs, out_specs, out_shape, aliases = [row_spec], [row_spec], [x_all_shape], {0: 0}
    n_main_out = len(out_specs)
    n_cast = len(cast_jobs)
    scratch = []
    if n_cast:
        for w, _, r in cast_jobs:
            assert w.shape[1] % r == 0 and w.shape[1] // r <= n_steps
        out_specs = out_specs + [any_spec] * n_cast
        out_shape = out_shape + [jax.ShapeDtypeStruct(w.shape[1:], _BF16) for w, _, _ in cast_jobs]
        scratch = ([pltpu.VMEM((2, r, w.shape[2]), _F32) for w, _, r in cast_jobs]
                   + [pltpu.VMEM((2, r, w.shape[2]), _BF16) for w, _, r in cast_jobs]
                   + [pltpu.SemaphoreType.DMA((2, n_cast)), pltpu.SemaphoreType.DMA((2, n_cast))])
    outs = pl.pallas_call(
        functools.partial(_ffn_kernel, mode=mode, n_prompt_blocks=npb, n_steps=n_steps,
                          cast_idx=tuple(idx for _, idx, _ in cast_jobs)),
        grid=(n_steps,),
        in_specs=x_specs + [_stacked(gains, g_idx), _stacked(gains, g_idx + 1)]
        + [_whole(w) for w in w16] + [any_spec] * n_cast,
        out_specs=out_specs,
        out_shape=out_shape,
        scratch_shapes=scratch,
        input_output_aliases=aliases,
        compiler_params=pltpu.CompilerParams(
            dimension_semantics=("arbitrary",), vmem_limit_bytes=FFN_VMEM_LIMIT_BYTES),
        name="ffn_" + mode,
    )(*xs, gains, gains, *w16, *[w for w, _, _ in cast_jobs])
    return outs[:n_main_out], list(outs[n_main_out:])


def _conv_body(x_ref, o_ref, newhist_ref, ext_ref, gpre_ref, gpost_ref, win_ref, ck_ref,
               wout_ref, *, rows, stride, pad, first_step, start_pos):
    del first_step, start_pos
    x = x_ref[...]
    h = _rmsnorm(x, gpre_ref[...]).astype(_BF16)
    gate_c = _dot(h, win_ref[:, D_MODEL:2 * D_MODEL])
    v = _dot(h, win_ref[:, 2 * D_MODEL:])
    z = gate_c * v
    ext_ref[pl.ds(pad, rows), :] = z
    ck = ck_ref[...]
    conv = (ck[0:1] * ext_ref[pl.ds(pad - 2 * stride, rows), :]
            + ck[1:2] * ext_ref[pl.ds(pad - stride, rows), :]
            + ck[2:3] * z)
    gate_b = _dot(h, win_ref[:, :D_MODEL])
    y = _dot((gate_b * conv).astype(_BF16), wout_ref[...])
    o_ref[...] = x + _rmsnorm(y, gpost_ref[...])

    tail = ext_ref[pl.ds(rows, pad), :]
    newhist_ref[...] = tail.reshape(newhist_ref.shape)
    ext_ref[pl.ds(0, pad), :] = tail


def _pool_body(x_ref, o_ref, newhist_ref, ext_ref, gpre_ref, gpost_ref, wgrp_ref, scale_ref,
               *, rows, stride, pad, first_step, start_pos):
    x = x_ref[...]
    u = _rmsnorm(x, gpre_ref[...])
    ext_ref[pl.ds(pad, rows), :] = u

    step = lax.broadcasted_iota(jnp.int32, (rows, 1), 0) // stride
    pos = step + first_step + start_pos

    ys = []
    for g, w in enumerate(POOL_WINDOWS):
        cols = slice(g * POOL_GROUP, (g + 1) * POOL_GROUP)
        acc = ext_ref[:, cols]
        span = 1
        while span < w:
            acc = acc + pltpu.roll(acc, span * stride, axis=0)
            span *= 2
        win_sum = acc[pad:, :]
        count = jnp.minimum(pos + 1, w).astype(_F32)
        diff = win_sum / count - u[:, cols]
        ys.append(_dot(diff.astype(_BF16), wgrp_ref[g]))
    y = jnp.concatenate(ys, axis=-1) * scale_ref[...]
    o_ref[...] = x + _rmsnorm(y, gpost_ref[...])

    tail = ext_ref[pl.ds(rows, pad), :]
    newhist_ref[...] = tail.reshape(newhist_ref.shape)
    ext_ref[pl.ds(0, pad), :] = tail


def _mixer_kernel(x_ref, hist_ref, *refs, body, n_params, rows, blocks_per_seq,
                  n_prompt_blocks, pad_p, pad_s, stride_s):
    params = refs[:n_params]
    o_ref, newhist_p_ref, newhist_s_ref, ext_p_ref, ext_s_ref = refs[n_params:]
    i = pl.program_id(0)

    @pl.when(i < n_prompt_blocks)
    def _():
        j = lax.rem(i, jnp.int32(blocks_per_seq))

        @pl.when(j == 0)
        def _():
            ext_p_ref[pl.ds(0, pad_p), :] = jnp.zeros((pad_p, D_MODEL), _F32)

        body(x_ref, o_ref, newhist_p_ref, ext_p_ref, *params, rows=rows, stride=1, pad=pad_p,
             first_step=j * rows, start_pos=0)

    @pl.when(i == n_prompt_blocks)
    def _():
        ext_s_ref[pl.ds(0, pad_s), :] = hist_ref[...]
        body(x_ref, o_ref, newhist_s_ref, ext_s_ref, *params, rows=rows, stride=stride_s,
             pad=pad_s, first_step=0, start_pos=PAST_LEN)


def _mixer_sublayer(body, name, x_all, hist_s, h_idx, params, *, n_seq, seq_len, n_sample,
                    stride_s, pad_p, pad_s):
    rows = MIX_ROWS
    assert seq_len % rows == 0 and n_sample == rows and hist_s.shape[1] == pad_s
    blocks_per_seq = seq_len // rows
    npb = n_seq * blocks_per_seq
    row_spec = pl.BlockSpec((rows, D_MODEL), lambda i: (i, 0))
    kernel = functools.partial(
        _mixer_kernel, body=body, n_params=len(params), rows=rows, blocks_per_seq=blocks_per_seq,
        n_prompt_blocks=npb, pad_p=pad_p, pad_s=pad_s, stride_s=stride_s)
    return pl.pallas_call(
        kernel,
        grid=(npb + 1,),
        in_specs=[row_spec, _stacked(hist_s, h_idx)] + [_stacked(a, k) for a, k in params],
        out_specs=[row_spec,
                   pl.BlockSpec((1, pad_p, D_MODEL),
                                lambda i: (jnp.minimum(i, npb - 1) // blocks_per_seq, 0, 0)),
                   pl.BlockSpec((pad_s, D_MODEL), lambda i: (0, 0))],
        out_shape=[jax.ShapeDtypeStruct(x_all.shape, _F32),
                   jax.ShapeDtypeStruct((n_seq, pad_p, D_MODEL), _F32),
                   jax.ShapeDtypeStruct((pad_s, D_MODEL), _F32)],
        scratch_shapes=[pltpu.VMEM((pad_p + rows, D_MODEL), _F32),
                        pltpu.VMEM((pad_s + rows, D_MODEL), _F32)],
        input_output_aliases={0: 0},
        compiler_params=pltpu.CompilerParams(
            dimension_semantics=("arbitrary",), vmem_limit_bytes=MIXER_VMEM_LIMIT_BYTES),
        name=name,
    )(x_all, hist_s, *[a for a, _ in params])


def kernel(x_prompt, x_sample, state_conv, state_pool, norm_gains, ffn_w_gate, ffn_w_up,
           ffn_w_down, conv_w_in, conv_kernel, conv_w_out, pool_w_group, pool_scale):
    batch, seq, _ = x_prompt.shape
    dec_batch, dec_seq, _ = x_sample.shape
    n_prompt = batch * seq
    n_sample = dec_batch * dec_seq
    n_conv = conv_w_in.shape[0]
    n_pool = pool_w_group.shape[0]

    xp_rows = x_prompt.reshape(n_prompt, D_MODEL)
    xs_rows = jnp.swapaxes(x_sample, 0, 1).reshape(n_sample, D_MODEL)

    gains = norm_gains.reshape(DEPTH * N_NORMS, 1, D_MODEL)
    ffn_w = (ffn_w_gate.reshape(2 * DEPTH, D_MODEL, D_FF),
             ffn_w_up.reshape(2 * DEPTH, D_MODEL, D_FF),
             ffn_w_down.reshape(2 * DEPTH, D_FF, D_MODEL))
    w_grp = pool_w_group.astype(_BF16)
    scale = pool_scale.reshape(n_pool, 1, D_MODEL)

    conv_pad_s = (CONV_W - 1) * dec_batch
    pool_pad_s = POOL_HIST * dec_batch
    conv_hist_s = jnp.swapaxes(state_conv, 1, 2).reshape(n_conv, conv_pad_s, D_MODEL)
    pool_hist_s = jnp.swapaxes(state_pool, 1, 2).reshape(n_pool, pool_pad_s, D_MODEL)
    ffn = functools.partial(_ffn_sublayer, n_prompt=n_prompt, n_sample=n_sample)
    w16 = [w[0].astype(_BF16) for w in ffn_w]
    ffn_chunk_rows = (CAST_ROWS_IN, CAST_ROWS_IN, CAST_ROWS_DOWN)

    def ffn_cast_jobs(idx):
        return [(w, idx, r) for w, r in zip(ffn_w, ffn_chunk_rows)]
    mixer = functools.partial(_mixer_sublayer, n_seq=batch, seq_len=seq, n_sample=n_sample,
                              stride_s=dec_batch)

    new_conv_p, new_conv_s, new_pool_p, new_pool_s = [], [], [], []
    for i in range(DEPTH):
        g0 = i * N_NORMS
        xs = (xp_rows, xs_rows) if i == 0 else (x_all,)
        j = i // 2
        jobs = ffn_cast_jobs(2 * i + 1)
        if i % 2 == 0:
            jobs += [(conv_w_in, j, CAST_ROWS_IN), (conv_w_out, j, CAST_ROWS_IN)]
        (x_all,), cast = ffn(xs, gains, g0, w16, jobs, mode="first" if i == 0 else "mid")
        w16, conv16 = cast[:3], cast[3:]
        norms = [(gains, g0 + 2), (gains, g0 + 3)]
        if i % 2 == 0:
            x_all, hp, hs = mixer(
                _conv_body, "conv_mixer", x_all, conv_hist_s, j,
                norms + [(conv16[0], None), (conv_kernel, j), (conv16[1], None)],
                pad_p=CONV_PAD_ROWS, pad_s=conv_pad_s)
            new_conv_p.append(hp[:, CONV_PAD_ROWS - (CONV_W - 1):, :])
            new_conv_s.append(jnp.swapaxes(hs.reshape(CONV_W - 1, dec_batch, D_MODEL), 0, 1))
        else:
            x_all, hp, hs = mixer(
                _pool_body, "pool_mixer", x_all, pool_hist_s, j,
                norms + [(w_grp, j), (scale, j)],
                pad_p=POOL_PAD_ROWS, pad_s=pool_pad_s)
            new_pool_p.append(hp[:, POOL_PAD_ROWS - POOL_HIST:, :])
            new_pool_s.append(jnp.swapaxes(hs.reshape(POOL_HIST, dec_batch, D_MODEL), 0, 1))
        if i == DEPTH - 1:
            (yp_rows, ys_rows), _ = ffn((x_all,), gains, g0 + 4, w16, [], mode="last")
        else:
            (x_all,), w16 = ffn((x_all,), gains, g0 + 4, w16, ffn_cast_jobs(2 * i + 2), mode="mid")

    y_prompt = yp_rows.reshape(batch, seq, D_MODEL)
    y_sample = jnp.swapaxes(ys_rows.reshape(dec_seq, dec_batch, D_MODEL), 0, 1)
    return (y_prompt, y_sample, jnp.stack(new_conv_p), jnp.stack(new_conv_s),
            jnp.stack(new_pool_p), jnp.stack(new_pool_s))
```

```python
import functools

import jax
import jax.numpy as jnp
from jax import lax
from jax.experimental import pallas as pl
from jax.experimental.pallas import tpu as pltpu

D_MODEL = 1024
D_FF = 2816
DEPTH = 4
N_NORMS = 6
CONV_W = 3
POOL_WINDOWS = (2, 4, 8, 16)
POOL_GROUP = D_MODEL // len(POOL_WINDOWS)
POOL_HIST = max(POOL_WINDOWS) - 1
PAST_LEN = 16384
EPS = 1e-6

SUBLANES = 8
FFN_ROWS = 512
FFN_SUB_ROWS = 256
CAST_ROWS_IN = 32
CAST_ROWS_DOWN = 176
MIX_ROWS = 512
CONV_PAD_ROWS = SUBLANES
POOL_PAD_ROWS = 2 * SUBLANES
FFN_VMEM_LIMIT_BYTES = 40 * 1024 * 1024
FFN_CAST_VMEM_LIMIT_BYTES = 56 * 1024 * 1024
FFN_F32_VMEM_LIMIT_BYTES = 60 * 1024 * 1024
MIXER_VMEM_LIMIT_BYTES = 32 * 1024 * 1024

_F32 = jnp.float32
_BF16 = jnp.bfloat16


def _rmsnorm(x, g):
    ms = jnp.mean(x * x, axis=-1, keepdims=True)
    return x * lax.rsqrt(ms + EPS) * g


def _dot(a, b):
    return jnp.dot(a, b, preferred_element_type=_F32)


def _whole(arr):
    nd = arr.ndim
    return pl.BlockSpec(arr.shape, lambda *_: (0,) * nd, pipeline_mode=pl.Buffered(1))


def _stacked(arr, idx):
    if idx is None:
        return _whole(arr)
    tail = arr.shape[1:]
    return pl.BlockSpec((None,) + tail, lambda *_: (idx,) + (0,) * len(tail),
                        pipeline_mode=pl.Buffered(1))


def _cast_side_job(i, last, jobs, dst_hbm, stage, ostage, sem_in, sem_out):
    slot = i & 1
    n_mats = len(jobs)
    chunk_rows = [st.shape[1] for st in stage]
    n_chunks = [dst.shape[0] // r for dst, r in zip(dst_hbm, chunk_rows)]
    by_count = {}
    for k, n in enumerate(n_chunks):
        by_count.setdefault(n, []).append(k)

    def rows_of(k, c):
        r = chunk_rows[k]
        return pl.ds(pl.multiple_of(c * r, r), r)

    def in_copy(k, c, s):
        src, idx = jobs[k]
        return pltpu.make_async_copy(src.at[idx, rows_of(k, c)], stage[k].at[s], sem_in.at[s, k])

    def out_copy(k, c, s):
        return pltpu.make_async_copy(ostage[k].at[s], dst_hbm[k].at[rows_of(k, c)],
                                     sem_out.at[s, k])

    def each(ks, fn):
        def run():
            for k in ks:
                fn(k)
        return run

    def prologue():
        pl.when(i == 0)(each(range(n_mats), lambda k: in_copy(k, 0, 0).start()))
        for n, ks in by_count.items():
            pl.when(i < n)(each(ks, lambda k: in_copy(k, i, slot).wait()))
            pl.when(i + 1 < n)(each(ks, lambda k: in_copy(k, i + 1, 1 - slot).start()))
            pl.when((i >= 2) & (i - 2 < n))(each(ks, lambda k: out_copy(k, i - 2, slot).wait()))

    def cast():
        for k in range(n_mats):
            ostage[k].at[slot][...] = stage[k].at[slot][...].astype(_BF16)

    def epilogue():
        for n, ks in by_count.items():
            pl.when(i < n)(each(ks, lambda k: out_copy(k, i, slot).start()))
            for c in range(max(last - 1, 0), n):
                assert c <= last
                pl.when(i == last)(each(ks, lambda k, c=c: out_copy(k, c, c & 1).wait()))

    return prologue, cast, epilogue


def _ffn_kernel(*refs, mode, n_prompt_blocks, n_steps, cast_idx):
    n_x = 2 if mode == "first" else 1
    n_out = 2 if mode == "last" else 1
    n_cast = len(cast_idx)
    x_refs, refs = refs[:n_x], refs[n_x:]
    (gpre_ref, gpost_ref, wg_ref, wu_ref, wd_ref), refs = refs[:5], refs[5:]
    src_hbm, refs = refs[:n_cast], refs[n_cast:]
    out_refs, refs = refs[:n_out], refs[n_out:]
    i = pl.program_id(0)
    is_prompt = i < n_prompt_blocks
    op_dtype = wg_ref.dtype
    sub_rows = FFN_SUB_ROWS if op_dtype == _BF16 else FFN_ROWS

    if n_cast:
        dst_hbm, stage, ostage = refs[:n_cast], refs[n_cast:2 * n_cast], refs[2 * n_cast:3 * n_cast]
        sem_in, sem_out = refs[3 * n_cast:]
        prologue, cast, epilogue = _cast_side_job(
            i, n_steps - 1, list(zip(src_hbm, cast_idx)), dst_hbm, stage, ostage, sem_in, sem_out)
        prologue()
        cast()

    for s in range(FFN_ROWS // sub_rows):
        rows = pl.ds(s * sub_rows, sub_rows)
        if mode == "first":
            x = jnp.where(is_prompt, x_refs[0][rows, :], x_refs[1][rows, :])
        else:
            x = x_refs[0][rows, :]
        h = _rmsnorm(x, gpre_ref[...]).astype(op_dtype)
        a = _dot(h, wg_ref[...])
        b = _dot(h, wu_ref[...])
        g = (a * jax.nn.sigmoid(a) * b).astype(op_dtype)
        y = _dot(g, wd_ref[...])
        out = x + _rmsnorm(y, 0.5 * gpost_ref[...])
        out_refs[0][rows, :] = out

    if mode == "last":
        @pl.when(i == 0)
        def _():
            out_refs[1][...] = out_refs[0][...]

    if n_cast:
        epilogue()


def _ffn_sublayer(xs, gains, g_idx, weights, cast_jobs, *, mode, n_prompt, n_sample):
    n = n_prompt + n_sample
    rows = FFN_ROWS
    assert n_prompt % rows == 0 and n_sample == rows
    n_steps = n // rows
    npb = n_prompt // rows
    row_spec = pl.BlockSpec((rows, D_MODEL), lambda i: (i, 0))
    prompt_spec = pl.BlockSpec((rows, D_MODEL), lambda i: (jnp.minimum(i, npb - 1), 0))
    sample_spec = pl.BlockSpec((rows, D_MODEL), lambda i: (0, 0))
    any_spec = pl.BlockSpec(memory_space=pl.ANY)
    x_all_shape = jax.ShapeDtypeStruct((n, D_MODEL), _F32)
    if mode == "first":
        x_specs, out_specs, out_shape, aliases = [prompt_spec, sample_spec], [row_spec], [x_all_shape], {}
    elif mode == "last":
        x_specs, aliases = [row_spec], {}
        out_specs = [prompt_spec, sample_spec]
        out_shape = [jax.ShapeDtypeStruct((n_prompt, D_MODEL), _F32),
                     jax.ShapeDtypeStruct((n_sample, D_MODEL), _F32)]
    else:
        x_specs, out_specs, out_shape, aliases = [row_spec], [row_spec], [x_all_shape], {0: 0}
    if mode == "last":
        x_specs = [pl.BlockSpec((rows, D_MODEL), lambda i: (jnp.where(i == 0, npb, i - 1), 0))]
        out_specs[0] = pl.BlockSpec((rows, D_MODEL), lambda i: (jnp.maximum(i - 1, 0), 0))
    n_main_out = len(out_specs)
    n_cast = len(cast_jobs)
    if weights[0][0].dtype == _F32:
        vmem_limit = FFN_F32_VMEM_LIMIT_BYTES
    else:
        vmem_limit = FFN_CAST_VMEM_LIMIT_BYTES if n_cast else FFN_VMEM_LIMIT_BYTES
    scratch = []
    if n_cast:
        for w, _, r in cast_jobs:
            assert w.shape[1] % r == 0 and w.shape[1] // r <= n_steps
        out_specs = out_specs + [any_spec] * n_cast
        out_shape = out_shape + [jax.ShapeDtypeStruct(w.shape[1:], _BF16) for w, _, _ in cast_jobs]
        scratch = ([pltpu.VMEM((2, r, w.shape[2]), _F32) for w, _, r in cast_jobs]
                   + [pltpu.VMEM((2, r, w.shape[2]), _BF16) for w, _, r in cast_jobs]
                   + [pltpu.SemaphoreType.DMA((2, n_cast)), pltpu.SemaphoreType.DMA((2, n_cast))])
    outs = pl.pallas_call(
        functools.partial(_ffn_kernel, mode=mode, n_prompt_blocks=npb, n_steps=n_steps,
                          cast_idx=tuple(idx for _, idx, _ in cast_jobs)),
        grid=(n_steps,),
        in_specs=x_specs + [_stacked(gains, g_idx), _stacked(gains, g_idx + 1)]
        + [_stacked(w, k) for w, k in weights] + [any_spec] * n_cast,
        out_specs=out_specs,
        out_shape=out_shape,
        scratch_shapes=scratch,
        input_output_aliases=aliases,
        compiler_params=pltpu.CompilerParams(
            dimension_semantics=("arbitrary",), vmem_limit_bytes=vmem_limit),
        name="ffn_" + mode,
    )(*xs, gains, gains, *[w for w, _ in weights], *[w for w, _, _ in cast_jobs])
    return outs[:n_main_out], list(outs[n_main_out:])


def _conv_body(x_ref, o_ref, newhist_ref, ext_ref, gpre_ref, gpost_ref, win_ref, ck_ref,
               wout_ref, *, rows, stride, pad, first_step, start_pos):
    del first_step, start_pos
    x = x_ref[...]
    h = _rmsnorm(x, gpre_ref[...]).astype(_BF16)
    gate_c = _dot(h, win_ref[:, D_MODEL:2 * D_MODEL])
    v = _dot(h, win_ref[:, 2 * D_MODEL:])
    z = gate_c * v
    ext_ref[pl.ds(pad, rows), :] = z
    ck = ck_ref[...]
    conv = (ck[0:1] * ext_ref[pl.ds(pad - 2 * stride, rows), :]
            + ck[1:2] * ext_ref[pl.ds(pad - stride, rows), :]
            + ck[2:3] * z)
    gate_b = _dot(h, win_ref[:, :D_MODEL])
    y = _dot((gate_b * conv).astype(_BF16), wout_ref[...])
    o_ref[...] = x + _rmsnorm(y, gpost_ref[...])

    tail = ext_ref[pl.ds(rows, pad), :]
    newhist_ref[...] = tail.reshape(newhist_ref.shape)
    ext_ref[pl.ds(0, pad), :] = tail


def _pool_body(x_ref, o_ref, newhist_ref, ext_ref, gpre_ref, gpost_ref, wgrp_ref, scale_ref,
               *, rows, stride, pad, first_step, start_pos):
    x = x_ref[...]
    u = _rmsnorm(x, gpre_ref[...])
    ext_ref[pl.ds(pad, rows), :] = u

    step = lax.broadcasted_iota(jnp.int32, (rows, 1), 0) // stride
    pos = step + first_step + start_pos

    ys = []
    for g, w in enumerate(POOL_WINDOWS):
        cols = slice(g * POOL_GROUP, (g + 1) * POOL_GROUP)
        acc = ext_ref[:, cols]
        span = 1
        while span < w:
            acc = acc + pltpu.roll(acc, span * stride, axis=0)
            span *= 2
        win_sum = acc[pad:, :]
        count = jnp.minimum(pos + 1, w).astype(_F32)
        diff = win_sum / count - u[:, cols]
        ys.append(_dot(diff.astype(_BF16), wgrp_ref[g]))
    y = jnp.concatenate(ys, axis=-1) * scale_ref[...]
    o_ref[...] = x + _rmsnorm(y, gpost_ref[...])

    tail = ext_ref[pl.ds(rows, pad), :]
    newhist_ref[...] = tail.reshape(newhist_ref.shape)
    ext_ref[pl.ds(0, pad), :] = tail


def _mixer_kernel(x_ref, hist_ref, *refs, body, n_params, rows, blocks_per_seq,
                  n_prompt_blocks, pad_p, pad_s, stride_s):
    params = refs[:n_params]
    o_ref, newhist_p_ref, newhist_s_ref, ext_p_ref, ext_s_ref = refs[n_params:]
    i = pl.program_id(0)

    @pl.when(i < n_prompt_blocks)
    def _():
        j = lax.rem(i, jnp.int32(blocks_per_seq))

        @pl.when(j == 0)
        def _():
            ext_p_ref[pl.ds(0, pad_p), :] = jnp.zeros((pad_p, D_MODEL), _F32)

        body(x_ref, o_ref, newhist_p_ref, ext_p_ref, *params, rows=rows, stride=1, pad=pad_p,
             first_step=j * rows, start_pos=0)

    @pl.when(i == n_prompt_blocks)
    def _():
        ext_s_ref[pl.ds(0, pad_s), :] = hist_ref[...]
        body(x_ref, o_ref, newhist_s_ref, ext_s_ref, *params, rows=rows, stride=stride_s,
             pad=pad_s, first_step=0, start_pos=PAST_LEN)


def _mixer_sublayer(body, name, x_all, hist_s, h_idx, params, *, n_seq, seq_len, n_sample,
                    stride_s, pad_p, pad_s):
    rows = MIX_ROWS
    assert seq_len % rows == 0 and n_sample == rows and hist_s.shape[1] == pad_s
    blocks_per_seq = seq_len // rows
    npb = n_seq * blocks_per_seq
    row_spec = pl.BlockSpec((rows, D_MODEL), lambda i: (i, 0))
    kernel = functools.partial(
        _mixer_kernel, body=body, n_params=len(params), rows=rows, blocks_per_seq=blocks_per_seq,
        n_prompt_blocks=npb, pad_p=pad_p, pad_s=pad_s, stride_s=stride_s)
    return pl.pallas_call(
        kernel,
        grid=(npb + 1,),
        in_specs=[row_spec, _stacked(hist_s, h_idx)] + [_stacked(a, k) for a, k in params],
        out_specs=[row_spec,
                   pl.BlockSpec((1, pad_p, D_MODEL),
                                lambda i: (jnp.minimum(i, npb - 1) // blocks_per_seq, 0, 0)),
                   pl.BlockSpec((pad_s, D_MODEL), lambda i: (0, 0))],
        out_shape=[jax.ShapeDtypeStruct(x_all.shape, _F32),
                   jax.ShapeDtypeStruct((n_seq, pad_p, D_MODEL), _F32),
                   jax.ShapeDtypeStruct((pad_s, D_MODEL), _F32)],
        scratch_shapes=[pltpu.VMEM((pad_p + rows, D_MODEL), _F32),
                        pltpu.VMEM((pad_s + rows, D_MODEL), _F32)],
        input_output_aliases={0: 0},
        compiler_params=pltpu.CompilerParams(
            dimension_semantics=("arbitrary",), vmem_limit_bytes=MIXER_VMEM_LIMIT_BYTES),
        name=name,
    )(x_all, hist_s, *[a for a, _ in params])


def kernel(x_prompt, x_sample, state_conv, state_pool, norm_gains, ffn_w_gate, ffn_w_up,
           ffn_w_down, conv_w_in, conv_kernel, conv_w_out, pool_w_group, pool_scale):
    batch, seq, _ = x_prompt.shape
    dec_batch, dec_seq, _ = x_sample.shape
    n_prompt = batch * seq
    n_sample = dec_batch * dec_seq
    n_conv = conv_w_in.shape[0]
    n_pool = pool_w_group.shape[0]

    xp_rows = x_prompt.reshape(n_prompt, D_MODEL)
    xs_rows = jnp.swapaxes(x_sample, 0, 1).reshape(n_sample, D_MODEL)

    gains = norm_gains.reshape(DEPTH * N_NORMS, 1, D_MODEL)
    ffn_w = (ffn_w_gate.reshape(2 * DEPTH, D_MODEL, D_FF),
             ffn_w_up.reshape(2 * DEPTH, D_MODEL, D_FF),
             ffn_w_down.reshape(2 * DEPTH, D_FF, D_MODEL))
    w_grp = pool_w_group.astype(_BF16)
    scale = pool_scale.reshape(n_pool, 1, D_MODEL)

    conv_pad_s = (CONV_W - 1) * dec_batch
    pool_pad_s = POOL_HIST * dec_batch
    conv_hist_s = jnp.swapaxes(state_conv, 1, 2).reshape(n_conv, conv_pad_s, D_MODEL)
    pool_hist_s = jnp.swapaxes(state_pool, 1, 2).reshape(n_pool, pool_pad_s, D_MODEL)
    ffn = functools.partial(_ffn_sublayer, n_prompt=n_prompt, n_sample=n_sample)
    weights = [(w, 0) for w in ffn_w]
    ffn_chunk_rows = (CAST_ROWS_IN, CAST_ROWS_IN, CAST_ROWS_DOWN)

    def ffn_cast_jobs(idx):
        return [(w, idx, r) for w, r in zip(ffn_w, ffn_chunk_rows)]
    mixer = functools.partial(_mixer_sublayer, n_seq=batch, seq_len=seq, n_sample=n_sample,
                              stride_s=dec_batch)

    new_conv_p, new_conv_s, new_pool_p, new_pool_s = [], [], [], []
    for i in range(DEPTH):
        g0 = i * N_NORMS
        xs = (xp_rows, xs_rows) if i == 0 else (x_all,)
        j = i // 2
        jobs = ffn_cast_jobs(2 * i + 1)
        if i % 2 == 0:
            jobs += [(conv_w_in, j, CAST_ROWS_IN), (conv_w_out, j, CAST_ROWS_IN)]
        (x_all,), cast = ffn(xs, gains, g0, weights, jobs, mode="first" if i == 0 else "mid")
        weights, conv16 = [(w, None) for w in cast[:3]], cast[3:]
        norms = [(gains, g0 + 2), (gains, g0 + 3)]
        if i % 2 == 0:
            x_all, hp, hs = mixer(
                _conv_body, "conv_mixer", x_all, conv_hist_s, j,
                norms + [(conv16[0], None), (conv_kernel, j), (conv16[1], None)],
                pad_p=CONV_PAD_ROWS, pad_s=conv_pad_s)
            new_conv_p.append(hp[:, CONV_PAD_ROWS - (CONV_W - 1):, :])
            new_conv_s.append(jnp.swapaxes(hs.reshape(CONV_W - 1, dec_batch, D_MODEL), 0, 1))
        else:
            x_all, hp, hs = mixer(
                _pool_body, "pool_mixer", x_all, pool_hist_s, j,
                norms + [(w_grp, j), (scale, j)],
                pad_p=POOL_PAD_ROWS, pad_s=pool_pad_s)
            new_pool_p.append(hp[:, POOL_PAD_ROWS - POOL_HIST:, :])
            new_pool_s.append(jnp.swapaxes(hs.reshape(POOL_HIST, dec_batch, D_MODEL), 0, 1))
        if i == DEPTH - 1:
            (yp_rows, ys_rows), _ = ffn((x_all,), gains, g0 + 4, weights, [], mode="last")
        else:
            (x_all,), cast = ffn((x_all,), gains, g0 + 4, weights, ffn_cast_jobs(2 * i + 2),
                                 mode="mid")
            weights = [(w, None) for w in cast]

    y_prompt = yp_rows.reshape(batch, seq, D_MODEL)
    y_sample = jnp.swapaxes(ys_rows.reshape(dec_seq, dec_batch, D_MODEL), 0, 1)
    return (y_prompt, y_sample, jnp.stack(new_conv_p), jnp.stack(new_conv_s),
            jnp.stack(new_pool_p), jnp.stack(new_pool_s))
```

```python
import functools

import jax
import jax.numpy as jnp
from jax import lax
from jax.experimental import pallas as pl
from jax.experimental.pallas import tpu as pltpu

D_MODEL = 1024
D_FF = 2816
DEPTH = 4
N_NORMS = 6
CONV_W = 3
POOL_WINDOWS = (2, 4, 8, 16)
POOL_GROUP = D_MODEL // len(POOL_WINDOWS)
POOL_HIST = max(POOL_WINDOWS) - 1
PAST_LEN = 16384
EPS = 1e-6

SUBLANES = 8
FFN_ROWS = 512
FFN_SUB_ROWS = 256
CAST_ROWS_IN = 32
CAST_ROWS_DOWN = 176
MIX_ROWS = 512
MIX_SUB_ROWS = 256
CONV_PAD_ROWS = SUBLANES
POOL_PAD_ROWS = 2 * SUBLANES
FFN_VMEM_LIMIT_BYTES = 40 * 1024 * 1024
FFN_CAST_VMEM_LIMIT_BYTES = 56 * 1024 * 1024
FFN_F32_VMEM_LIMIT_BYTES = 60 * 1024 * 1024
MIXER_VMEM_LIMIT_BYTES = 32 * 1024 * 1024

_F32 = jnp.float32
_BF16 = jnp.bfloat16


def _rmsnorm(x, g):
    ms = jnp.mean(x * x, axis=-1, keepdims=True)
    return x * lax.rsqrt(ms + EPS) * g


def _dot(a, b):
    return jnp.dot(a, b, preferred_element_type=_F32)


def _whole(arr):
    nd = arr.ndim
    return pl.BlockSpec(arr.shape, lambda *_: (0,) * nd, pipeline_mode=pl.Buffered(1))


def _stacked(arr, idx):
    if idx is None:
        return _whole(arr)
    tail = arr.shape[1:]
    return pl.BlockSpec((None,) + tail, lambda *_: (idx,) + (0,) * len(tail),
                        pipeline_mode=pl.Buffered(1))


def _cast_side_job(i, last, jobs, dst_hbm, stage, ostage, sem_in, sem_out):
    slot = i & 1
    n_mats = len(jobs)
    chunk_rows = [st.shape[1] for st in stage]
    n_chunks = [dst.shape[0] // r for dst, r in zip(dst_hbm, chunk_rows)]
    by_count = {}
    for k, n in enumerate(n_chunks):
        by_count.setdefault(n, []).append(k)

    def rows_of(k, c):
        r = chunk_rows[k]
        return pl.ds(pl.multiple_of(c * r, r), r)

    def in_copy(k, c, s):
        src, idx = jobs[k]
        return pltpu.make_async_copy(src.at[idx, rows_of(k, c)], stage[k].at[s], sem_in.at[s, k])

    def out_copy(k, c, s):
        return pltpu.make_async_copy(ostage[k].at[s], dst_hbm[k].at[rows_of(k, c)],
                                     sem_out.at[s, k])

    def each(ks, fn):
        def run():
            for k in ks:
                fn(k)
        return run

    def prologue():
        pl.when(i == 0)(each(range(n_mats), lambda k: in_copy(k, 0, 0).start()))
        for n, ks in by_count.items():
            pl.when(i < n)(each(ks, lambda k: in_copy(k, i, slot).wait()))
            pl.when(i + 1 < n)(each(ks, lambda k: in_copy(k, i + 1, 1 - slot).start()))
            pl.when((i >= 2) & (i - 2 < n))(each(ks, lambda k: out_copy(k, i - 2, slot).wait()))

    def cast():
        for k in range(n_mats):
            ostage[k].at[slot][...] = stage[k].at[slot][...].astype(_BF16)

    def epilogue():
        for n, ks in by_count.items():
            pl.when(i < n)(each(ks, lambda k: out_copy(k, i, slot).start()))
            for c in range(max(last - 1, 0), n):
                assert c <= last
                pl.when(i == last)(each(ks, lambda k, c=c: out_copy(k, c, c & 1).wait()))

    return prologue, cast, epilogue


def _ffn_kernel(*refs, mode, n_prompt_blocks, n_steps, cast_idx):
    n_x = 2 if mode == "first" else 1
    n_out = 2 if mode == "last" else 1
    n_cast = len(cast_idx)
    x_refs, refs = refs[:n_x], refs[n_x:]
    (gpre_ref, gpost_ref, wg_ref, wu_ref, wd_ref), refs = refs[:5], refs[5:]
    src_hbm, refs = refs[:n_cast], refs[n_cast:]
    out_refs, refs = refs[:n_out], refs[n_out:]
    i = pl.program_id(0)
    is_prompt = i < n_prompt_blocks
    op_dtype = wg_ref.dtype
    sub_rows = FFN_SUB_ROWS if op_dtype == _BF16 else FFN_ROWS

    if n_cast:
        dst_hbm, stage, ostage = refs[:n_cast], refs[n_cast:2 * n_cast], refs[2 * n_cast:3 * n_cast]
        sem_in, sem_out = refs[3 * n_cast:]
        prologue, cast, epilogue = _cast_side_job(
            i, n_steps - 1, list(zip(src_hbm, cast_idx)), dst_hbm, stage, ostage, sem_in, sem_out)
        prologue()
        cast()

    for s in range(FFN_ROWS // sub_rows):
        rows = pl.ds(s * sub_rows, sub_rows)
        if mode == "first":
            x = jnp.where(is_prompt, x_refs[0][rows, :], x_refs[1][rows, :])
        else:
            x = x_refs[0][rows, :]
        h = _rmsnorm(x, gpre_ref[...]).astype(op_dtype)
        a = _dot(h, wg_ref[...])
        b = _dot(h, wu_ref[...])
        g = (a * jax.nn.sigmoid(a) * b).astype(op_dtype)
        y = _dot(g, wd_ref[...])
        out = x + _rmsnorm(y, 0.5 * gpost_ref[...])
        out_refs[0][rows, :] = out

    if mode == "last":
        @pl.when(i == 0)
        def _():
            out_refs[1][...] = out_refs[0][...]

    if n_cast:
        epilogue()


def _ffn_sublayer(xs, gains, g_idx, weights, cast_jobs, *, mode, n_prompt, n_sample):
    n = n_prompt + n_sample
    rows = FFN_ROWS
    assert n_prompt % rows == 0 and n_sample == rows
    n_steps = n // rows
    npb = n_prompt // rows
    row_spec = pl.BlockSpec((rows, D_MODEL), lambda i: (i, 0))
    prompt_spec = pl.BlockSpec((rows, D_MODEL), lambda i: (jnp.minimum(i, npb - 1), 0))
    sample_spec = pl.BlockSpec((rows, D_MODEL), lambda i: (0, 0))
    any_spec = pl.BlockSpec(memory_space=pl.ANY)
    x_all_shape = jax.ShapeDtypeStruct((n, D_MODEL), _F32)
    if mode == "first":
        x_specs, out_specs, out_shape, aliases = [prompt_spec, sample_spec], [row_spec], [x_all_shape], {}
    elif mode == "last":
        x_specs, aliases = [row_spec], {}
        out_specs = [prompt_spec, sample_spec]
        out_shape = [jax.ShapeDtypeStruct((n_prompt, D_MODEL), _F32),
                     jax.ShapeDtypeStruct((n_sample, D_MODEL), _F32)]
    else:
        x_specs, out_specs, out_shape, aliases = [row_spec], [row_spec], [x_all_shape], {0: 0}
    if mode == "last":
        x_specs = [pl.BlockSpec((rows, D_MODEL), lambda i: (jnp.where(i == 0, npb, i - 1), 0))]
        out_specs[0] = pl.BlockSpec((rows, D_MODEL), lambda i: (jnp.maximum(i - 1, 0), 0))
    n_main_out = len(out_specs)
    n_cast = len(cast_jobs)
    if weights[0][0].dtype == _F32:
        vmem_limit = FFN_F32_VMEM_LIMIT_BYTES
    else:
        vmem_limit = FFN_CAST_VMEM_LIMIT_BYTES if n_cast else FFN_VMEM_LIMIT_BYTES
    scratch = []
    if n_cast:
        for w, _, r in cast_jobs:
            assert w.shape[1] % r == 0 and w.shape[1] // r <= n_steps
        out_specs = out_specs + [any_spec] * n_cast
        out_shape = out_shape + [jax.ShapeDtypeStruct(w.shape[1:], _BF16) for w, _, _ in cast_jobs]
        scratch = ([pltpu.VMEM((2, r, w.shape[2]), _F32) for w, _, r in cast_jobs]
                   + [pltpu.VMEM((2, r, w.shape[2]), _BF16) for w, _, r in cast_jobs]
                   + [pltpu.SemaphoreType.DMA((2, n_cast)), pltpu.SemaphoreType.DMA((2, n_cast))])
    outs = pl.pallas_call(
        functools.partial(_ffn_kernel, mode=mode, n_prompt_blocks=npb, n_steps=n_steps,
                          cast_idx=tuple(idx for _, idx, _ in cast_jobs)),
        grid=(n_steps,),
        in_specs=x_specs + [_stacked(gains, g_idx), _stacked(gains, g_idx + 1)]
        + [_stacked(w, k) for w, k in weights] + [any_spec] * n_cast,
        out_specs=out_specs,
        out_shape=out_shape,
        scratch_shapes=scratch,
        input_output_aliases=aliases,
        compiler_params=pltpu.CompilerParams(
            dimension_semantics=("arbitrary",), vmem_limit_bytes=vmem_limit),
        name="ffn_" + mode,
    )(*xs, gains, gains, *[w for w, _ in weights], *[w for w, _, _ in cast_jobs])
    return outs[:n_main_out], list(outs[n_main_out:])


def _conv_body(x_ref, o_ref, newhist_ref, ext_ref, gpre_ref, gpost_ref, win_ref, ck_ref,
               wout_ref, *, rows, stride, pad, first_step, start_pos):
    del first_step, start_pos
    ck = ck_ref[...]
    sub = MIX_SUB_ROWS
    assert rows % sub == 0 and sub % stride == 0
    for r0 in range(0, rows, sub):
        x = x_ref[pl.ds(r0, sub), :]
        h = _rmsnorm(x, gpre_ref[...]).astype(_BF16)
        gate_c = _dot(h, win_ref[:, D_MODEL:2 * D_MODEL])
        v = _dot(h, win_ref[:, 2 * D_MODEL:])
        z = gate_c * v
        ext_ref[pl.ds(pad + r0, sub), :] = z
        conv = (ck[0:1] * ext_ref[pl.ds(pad + r0 - 2 * stride, sub), :]
                + ck[1:2] * ext_ref[pl.ds(pad + r0 - stride, sub), :]
                + ck[2:3] * z)
        gate_b = _dot(h, win_ref[:, :D_MODEL])
        y = _dot((gate_b * conv).astype(_BF16), wout_ref[...])
        o_ref[pl.ds(r0, sub), :] = x + _rmsnorm(y, gpost_ref[...])

    tail = ext_ref[pl.ds(rows, pad), :]
    newhist_ref[...] = tail.reshape(newhist_ref.shape)
    ext_ref[pl.ds(0, pad), :] = tail


def _pool_body(x_ref, o_ref, newhist_ref, ext_ref, gpre_ref, gpost_ref, wgrp_ref, scale_ref,
               *, rows, stride, pad, first_step, start_pos):
    x = x_ref[...]
    u = _rmsnorm(x, gpre_ref[...])
    ext_ref[pl.ds(pad, rows), :] = u

    step = lax.broadcasted_iota(jnp.int32, (rows, 1), 0) // stride
    pos = step + first_step + start_pos

    ys = []
    for g, w in enumerate(POOL_WINDOWS):
        cols = slice(g * POOL_GROUP, (g + 1) * POOL_GROUP)
        acc = ext_ref[:, cols]
        span = 1
        while span < w:
            acc = acc + pltpu.roll(acc, span * stride, axis=0)
            span *= 2
        win_sum = acc[pad:, :]
        count = jnp.minimum(pos + 1, w).astype(_F32)
        diff = win_sum / count - u[:, cols]
        ys.append(_dot(diff, wgrp_ref[g]))
    y = jnp.concatenate(ys, axis=-1) * scale_ref[...]
    o_ref[...] = x + _rmsnorm(y, gpost_ref[...])

    tail = ext_ref[pl.ds(rows, pad), :]
    newhist_ref[...] = tail.reshape(newhist_ref.shape)
    ext_ref[pl.ds(0, pad), :] = tail


def _mixer_kernel(x_ref, hist_ref, *refs, body, n_params, rows, blocks_per_seq,
                  n_prompt_blocks, pad_p, pad_s, stride_s):
    params = refs[:n_params]
    o_ref, newhist_p_ref, newhist_s_ref, ext_p_ref, ext_s_ref = refs[n_params:]
    i = pl.program_id(0)

    @pl.when(i < n_prompt_blocks)
    def _():
        j = lax.rem(i, jnp.int32(blocks_per_seq))

        @pl.when(j == 0)
        def _():
            ext_p_ref[pl.ds(0, pad_p), :] = jnp.zeros((pad_p, D_MODEL), _F32)

        body(x_ref, o_ref, newhist_p_ref, ext_p_ref, *params, rows=rows, stride=1, pad=pad_p,
             first_step=j * rows, start_pos=0)

    @pl.when(i == n_prompt_blocks)
    def _():
        ext_s_ref[pl.ds(0, pad_s), :] = hist_ref[...]
        body(x_ref, o_ref, newhist_s_ref, ext_s_ref, *params, rows=rows, stride=stride_s,
             pad=pad_s, first_step=0, start_pos=PAST_LEN)


def _mixer_sublayer(body, name, x_all, hist_s, h_idx, params, *, n_seq, seq_len, n_sample,
                    stride_s, pad_p, pad_s):
    rows = MIX_ROWS
    assert seq_len % rows == 0 and n_sample == rows and hist_s.shape[1] == pad_s
    blocks_per_seq = seq_len // rows
    npb = n_seq * blocks_per_seq
    row_spec = pl.BlockSpec((rows, D_MODEL), lambda i: (i, 0))
    kernel = functools.partial(
        _mixer_kernel, body=body, n_params=len(params), rows=rows, blocks_per_seq=blocks_per_seq,
        n_prompt_blocks=npb, pad_p=pad_p, pad_s=pad_s, stride_s=stride_s)
    return pl.pallas_call(
        kernel,
        grid=(npb + 1,),
        in_specs=[row_spec, _stacked(hist_s, h_idx)] + [_stacked(a, k) for a, k in params],
        out_specs=[row_spec,
                   pl.BlockSpec((1, pad_p, D_MODEL),
                                lambda i: (jnp.minimum(i, npb - 1) // blocks_per_seq, 0, 0)),
                   pl.BlockSpec((pad_s, D_MODEL), lambda i: (0, 0))],
        out_shape=[jax.ShapeDtypeStruct(x_all.shape, _F32),
                   jax.ShapeDtypeStruct((n_seq, pad_p, D_MODEL), _F32),
                   jax.ShapeDtypeStruct((pad_s, D_MODEL), _F32)],
        scratch_shapes=[pltpu.VMEM((pad_p + rows, D_MODEL), _F32),
                        pltpu.VMEM((pad_s + rows, D_MODEL), _F32)],
        input_output_aliases={0: 0},
        compiler_params=pltpu.CompilerParams(
            dimension_semantics=("arbitrary",), vmem_limit_bytes=MIXER_VMEM_LIMIT_BYTES),
        name=name,
    )(x_all, hist_s, *[a for a, _ in params])


def kernel(x_prompt, x_sample, state_conv, state_pool, norm_gains, ffn_w_gate, ffn_w_up,
           ffn_w_down, conv_w_in, conv_kernel, conv_w_out, pool_w_group, pool_scale):
    batch, seq, _ = x_prompt.shape
    dec_batch, dec_seq, _ = x_sample.shape
    n_prompt = batch * seq
    n_sample = dec_batch * dec_seq
    n_conv = conv_w_in.shape[0]
    n_pool = pool_w_group.shape[0]

    xp_rows = x_prompt.reshape(n_prompt, D_MODEL)
    xs_rows = jnp.swapaxes(x_sample, 0, 1).reshape(n_sample, D_MODEL)

    gains = norm_gains.reshape(DEPTH * N_NORMS, 1, D_MODEL)
    ffn_w = (ffn_w_gate.reshape(2 * DEPTH, D_MODEL, D_FF),
             ffn_w_up.reshape(2 * DEPTH, D_MODEL, D_FF),
             ffn_w_down.reshape(2 * DEPTH, D_FF, D_MODEL))
    scale = pool_scale.reshape(n_pool, 1, D_MODEL)

    conv_pad_s = (CONV_W - 1) * dec_batch
    pool_pad_s = POOL_HIST * dec_batch
    conv_hist_s = jnp.swapaxes(state_conv, 1, 2).reshape(n_conv, conv_pad_s, D_MODEL)
    pool_hist_s = jnp.swapaxes(state_pool, 1, 2).reshape(n_pool, pool_pad_s, D_MODEL)
    ffn = functools.partial(_ffn_sublayer, n_prompt=n_prompt, n_sample=n_sample)
    weights = [(w, 0) for w in ffn_w]
    ffn_chunk_rows = (CAST_ROWS_IN, CAST_ROWS_IN, CAST_ROWS_DOWN)

    def ffn_cast_jobs(idx):
        return [(w, idx, r) for w, r in zip(ffn_w, ffn_chunk_rows)]
    mixer = functools.partial(_mixer_sublayer, n_seq=batch, seq_len=seq, n_sample=n_sample,
                              stride_s=dec_batch)

    new_conv_p, new_conv_s, new_pool_p, new_pool_s = [], [], [], []
    for i in range(DEPTH):
        g0 = i * N_NORMS
        xs = (xp_rows, xs_rows) if i == 0 else (x_all,)
        j = i // 2
        jobs = ffn_cast_jobs(2 * i + 1)
        if i % 2 == 0:
            jobs += [(conv_w_in, j, CAST_ROWS_IN), (conv_w_out, j, CAST_ROWS_IN)]
        (x_all,), cast = ffn(xs, gains, g0, weights, jobs, mode="first" if i == 0 else "mid")
        weights, conv16 = [(w, None) for w in cast[:3]], cast[3:]
        norms = [(gains, g0 + 2), (gains, g0 + 3)]
        if i % 2 == 0:
            x_all, hp, hs = mixer(
                _conv_body, "conv_mixer", x_all, conv_hist_s, j,
                norms + [(conv16[0], None), (conv_kernel, j), (conv16[1], None)],
                pad_p=CONV_PAD_ROWS, pad_s=conv_pad_s)
            new_conv_p.append(hp[:, CONV_PAD_ROWS - (CONV_W - 1):, :])
            new_conv_s.append(jnp.swapaxes(hs.reshape(CONV_W - 1, dec_batch, D_MODEL), 0, 1))
        else:
            x_all, hp, hs = mixer(
                _pool_body, "pool_mixer", x_all, pool_hist_s, j,
                norms + [(pool_w_group, j), (scale, j)],
                pad_p=POOL_PAD_ROWS, pad_s=pool_pad_s)
            new_pool_p.append(hp[:, POOL_PAD_ROWS - POOL_HIST:, :])
            new_pool_s.append(jnp.swapaxes(hs.reshape(POOL_HIST, dec_batch, D_MODEL), 0, 1))
        if i == DEPTH - 1:
            (yp_rows, ys_rows), _ = ffn((x_all,), gains, g0 + 4, weights, [], mode="last")
        else:
            (x_all,), cast = ffn((x_all,), gains, g0 + 4, weights, ffn_cast_jobs(2 * i + 2),
                                 mode="mid")
            weights = [(w, None) for w in cast]

    y_prompt = yp_rows.reshape(batch, seq, D_MODEL)
    y_sample = jnp.swapaxes(ys_rows.reshape(dec_seq, dec_batch, D_MODEL), 0, 1)
    return (y_prompt, y_sample, jnp.stack(new_conv_p), jnp.stack(new_conv_s),
            jnp.stack(new_pool_p), jnp.stack(new_pool_s))
```

```python
import functools

import jax
import jax.numpy as jnp
from jax import lax
from jax.experimental import pallas as pl
from jax.experimental.pallas import tpu as pltpu

D_MODEL = 1024
D_FF = 2816
DEPTH = 4
N_NORMS = 6
CONV_W = 3
POOL_WINDOWS = (2, 4, 8, 16)
POOL_GROUP = D_MODEL // len(POOL_WINDOWS)
POOL_HIST = max(POOL_WINDOWS) - 1
PAST_LEN = 16384
EPS = 1e-6

SUBLANES = 8
FFN_ROWS = 512
FFN_SUB_ROWS = 256
CAST_ROWS_IN = 32
CAST_ROWS_DOWN = 176
MIX_ROWS = 512
CONV_PAD_ROWS = SUBLANES
POOL_PAD_ROWS = 2 * SUBLANES
FFN_VMEM_LIMIT_BYTES = 40 * 1024 * 1024
FFN_CAST_VMEM_LIMIT_BYTES = 56 * 1024 * 1024
FFN_F32_VMEM_LIMIT_BYTES = 60 * 1024 * 1024
MIXER_VMEM_LIMIT_BYTES = 32 * 1024 * 1024

_F32 = jnp.float32
_BF16 = jnp.bfloat16


def _rmsnorm(x, g):
    ms = jnp.mean(x * x, axis=-1, keepdims=True)
    return x * lax.rsqrt(ms + EPS) * g


def _dot(a, b):
    return jnp.dot(a, b, preferred_element_type=_F32)


def _whole(arr):
    nd = arr.ndim
    return pl.BlockSpec(arr.shape, lambda *_: (0,) * nd, pipeline_mode=pl.Buffered(1))


def _stacked(arr, idx):
    if idx is None:
        return _whole(arr)
    tail = arr.shape[1:]
    return pl.BlockSpec((None,) + tail, lambda *_: (idx,) + (0,) * len(tail),
                        pipeline_mode=pl.Buffered(1))


def _cast_side_job(i, last, jobs, dst_hbm, stage, ostage, sem_in, sem_out):
    slot = i & 1
    n_mats = len(jobs)
    chunk_rows = [st.shape[1] for st in stage]
    n_chunks = [dst.shape[0] // r for dst, r in zip(dst_hbm, chunk_rows)]
    by_count = {}
    for k, n in enumerate(n_chunks):
        by_count.setdefault(n, []).append(k)

    def rows_of(k, c):
        r = chunk_rows[k]
        return pl.ds(pl.multiple_of(c * r, r), r)

    def in_copy(k, c, s):
        src, idx = jobs[k]
        return pltpu.make_async_copy(src.at[idx, rows_of(k, c)], stage[k].at[s], sem_in.at[s, k])

    def out_copy(k, c, s):
        return pltpu.make_async_copy(ostage[k].at[s], dst_hbm[k].at[rows_of(k, c)],
                                     sem_out.at[s, k])

    def each(ks, fn):
        def run():
            for k in ks:
                fn(k)
        return run

    def prologue():
        pl.when(i == 0)(each(range(n_mats), lambda k: in_copy(k, 0, 0).start()))
        for n, ks in by_count.items():
            pl.when(i < n)(each(ks, lambda k: in_copy(k, i, slot).wait()))
            pl.when(i + 1 < n)(each(ks, lambda k: in_copy(k, i + 1, 1 - slot).start()))
            pl.when((i >= 2) & (i - 2 < n))(each(ks, lambda k: out_copy(k, i - 2, slot).wait()))

    def cast():
        for k in range(n_mats):
            ostage[k].at[slot][...] = stage[k].at[slot][...].astype(_BF16)

    def epilogue():
        for n, ks in by_count.items():
            pl.when(i < n)(each(ks, lambda k: out_copy(k, i, slot).start()))
            for c in range(max(last - 1, 0), n):
                assert c <= last
                pl.when(i == last)(each(ks, lambda k, c=c: out_copy(k, c, c & 1).wait()))

    return prologue, cast, epilogue


def _ffn_kernel(*refs, mode, n_prompt_blocks, n_steps, cast_idx):
    n_x = 2 if mode == "first" else 1
    n_out = 2 if mode == "last" else 1
    n_cast = len(cast_idx)
    x_refs, refs = refs[:n_x], refs[n_x:]
    (gpre_ref, gpost_ref, wg_ref, wu_ref, wd_ref), refs = refs[:5], refs[5:]
    src_hbm, refs = refs[:n_cast], refs[n_cast:]
    out_refs, refs = refs[:n_out], refs[n_out:]
    i = pl.program_id(0)
    is_prompt = i < n_prompt_blocks
    op_dtype = wg_ref.dtype
    sub_rows = FFN_SUB_ROWS if op_dtype == _BF16 else FFN_ROWS

    if n_cast:
        dst_hbm, stage, ostage = refs[:n_cast], refs[n_cast:2 * n_cast], refs[2 * n_cast:3 * n_cast]
        sem_in, sem_out = refs[3 * n_cast:]
        prologue, cast, epilogue = _cast_side_job(
            i, n_steps - 1, list(zip(src_hbm, cast_idx)), dst_hbm, stage, ostage, sem_in, sem_out)
        prologue()
        cast()

    for s in range(FFN_ROWS // sub_rows):
        rows = pl.ds(s * sub_rows, sub_rows)
        if mode == "first":
            x = jnp.where(is_prompt, x_refs[0][rows, :], x_refs[1][rows, :])
        else:
            x = x_refs[0][rows, :]
        h = _rmsnorm(x, gpre_ref[...]).astype(op_dtype)
        a = _dot(h, wg_ref[...])
        b = _dot(h, wu_ref[...])
        g = (a * jax.nn.sigmoid(a) * b).astype(op_dtype)
        y = _dot(g, wd_ref[...])
        out = x + _rmsnorm(y, 0.5 * gpost_ref[...])
        out_refs[0][rows, :] = out

    if mode == "last":
        @pl.when(i == 0)
        def _():
            out_refs[1][...] = out_refs[0][...]

    if n_cast:
        epilogue()


def _ffn_sublayer(xs, gains, g_idx, weights, cast_jobs, *, mode, n_prompt, n_sample):
    n = n_prompt + n_sample
    rows = FFN_ROWS
    assert n_prompt % rows == 0 and n_sample == rows
    n_steps = n // rows
    npb = n_prompt // rows
    row_spec = pl.BlockSpec((rows, D_MODEL), lambda i: (i, 0))
    prompt_spec = pl.BlockSpec((rows, D_MODEL), lambda i: (jnp.minimum(i, npb - 1), 0))
    sample_spec = pl.BlockSpec((rows, D_MODEL), lambda i: (0, 0))
    any_spec = pl.BlockSpec(memory_space=pl.ANY)
    x_all_shape = jax.ShapeDtypeStruct((n, D_MODEL), _F32)
    if mode == "first":
        x_specs, out_specs, out_shape, aliases = [prompt_spec, sample_spec], [row_spec], [x_all_shape], {}
    elif mode == "last":
        x_specs, aliases = [row_spec], {}
        out_specs = [prompt_spec, sample_spec]
        out_shape = [jax.ShapeDtypeStruct((n_prompt, D_MODEL), _F32),
                     jax.ShapeDtypeStruct((n_sample, D_MODEL), _F32)]
    else:
        x_specs, out_specs, out_shape, aliases = [row_spec], [row_spec], [x_all_shape], {0: 0}
    if mode == "last":
        x_specs = [pl.BlockSpec((rows, D_MODEL), lambda i: (jnp.where(i == 0, npb, i - 1), 0))]
        out_specs[0] = pl.BlockSpec((rows, D_MODEL), lambda i: (jnp.maximum(i - 1, 0), 0))
    n_main_out = len(out_specs)
    n_cast = len(cast_jobs)
    if weights[0][0].dtype == _F32:
        vmem_limit = FFN_F32_VMEM_LIMIT_BYTES
    else:
        vmem_limit = FFN_CAST_VMEM_LIMIT_BYTES if n_cast else FFN_VMEM_LIMIT_BYTES
    scratch = []
    if n_cast:
        for w, _, r in cast_jobs:
            assert w.shape[1] % r == 0 and w.shape[1] // r <= n_steps
        out_specs = out_specs + [any_spec] * n_cast
        out_shape = out_shape + [jax.ShapeDtypeStruct(w.shape[1:], _BF16) for w, _, _ in cast_jobs]
        scratch = ([pltpu.VMEM((2, r, w.shape[2]), _F32) for w, _, r in cast_jobs]
                   + [pltpu.VMEM((2, r, w.shape[2]), _BF16) for w, _, r in cast_jobs]
                   + [pltpu.SemaphoreType.DMA((2, n_cast)), pltpu.SemaphoreType.DMA((2, n_cast))])
    outs = pl.pallas_call(
        functools.partial(_ffn_kernel, mode=mode, n_prompt_blocks=npb, n_steps=n_steps,
                          cast_idx=tuple(idx for _, idx, _ in cast_jobs)),
        grid=(n_steps,),
        in_specs=x_specs + [_stacked(gains, g_idx), _stacked(gains, g_idx + 1)]
        + [_stacked(w, k) for w, k in weights] + [any_spec] * n_cast,
        out_specs=out_specs,
        out_shape=out_shape,
        scratch_shapes=scratch,
        input_output_aliases=aliases,
        compiler_params=pltpu.CompilerParams(
            dimension_semantics=("arbitrary",), vmem_limit_bytes=vmem_limit),
        name="ffn_" + mode,
    )(*xs, gains, gains, *[w for w, _ in weights], *[w for w, _, _ in cast_jobs])
    return outs[:n_main_out], list(outs[n_main_out:])


def _conv_body(x_ref, o_ref, newhist_ref, ext_ref, gpre_ref, gpost_ref, win_ref, ck_ref,
               wout_ref, *, rows, stride, pad, first_step, start_pos):
    del first_step, start_pos
    x = x_ref[...]
    h = _rmsnorm(x, gpre_ref[...]).astype(_BF16)
    gate_c = _dot(h, win_ref[:, D_MODEL:2 * D_MODEL])
    v = _dot(h, win_ref[:, 2 * D_MODEL:])
    z = gate_c * v
    ext_ref[pl.ds(pad, rows), :] = z
    ck = ck_ref[...]
    conv = (ck[0:1] * ext_ref[pl.ds(pad - 2 * stride, rows), :]
            + ck[1:2] * ext_ref[pl.ds(pad - stride, rows), :]
            + ck[2:3] * z)
    gate_b = _dot(h, win_ref[:, :D_MODEL])
    y = _dot((gate_b * conv).astype(_BF16), wout_ref[...])
    o_ref[...] = x + _rmsnorm(y, gpost_ref[...])

    tail = ext_ref[pl.ds(rows, pad), :]
    newhist_ref[...] = tail.reshape(newhist_ref.shape)
    ext_ref[pl.ds(0, pad), :] = tail


def _pool_body(x_ref, o_ref, newhist_ref, ext_ref, gpre_ref, gpost_ref, wgrp_ref, scale_ref,
               *, rows, stride, pad, first_step, start_pos):
    x = x_ref[...]
    u = _rmsnorm(x, gpre_ref[...])
    ext_ref[pl.ds(pad, rows), :] = u

    step = lax.broadcasted_iota(jnp.int32, (rows, 1), 0) // stride
    pos = step + first_step + start_pos

    ys = []
    for g, w in enumerate(POOL_WINDOWS):
        cols = slice(g * POOL_GROUP, (g + 1) * POOL_GROUP)
        acc = ext_ref[:, cols]
        span = 1
        while span < w:
            acc = acc + pltpu.roll(acc, span * stride, axis=0)
            span *= 2
        win_sum = acc[pad:, :]
        count = jnp.minimum(pos + 1, w).astype(_F32)
        diff = win_sum / count - u[:, cols]
        ys.append(_dot(diff, wgrp_ref[g]))
    y = jnp.concatenate(ys, axis=-1) * scale_ref[...]
    o_ref[...] = x + _rmsnorm(y, gpost_ref[...])

    tail = ext_ref[pl.ds(rows, pad), :]
    newhist_ref[...] = tail.reshape(newhist_ref.shape)
    ext_ref[pl.ds(0, pad), :] = tail


def _mixer_kernel(x_ref, hist_ref, *refs, body, n_params, rows, blocks_per_seq,
                  n_prompt_blocks, pad_p, pad_s, stride_s):
    params = refs[:n_params]
    o_ref, newhist_p_ref, newhist_s_ref, ext_p_ref, ext_s_ref = refs[n_params:]
    i = pl.program_id(0)

    @pl.when(i < n_prompt_blocks)
    def _():
        j = lax.rem(i, jnp.int32(blocks_per_seq))

        @pl.when(j == 0)
        def _():
            ext_p_ref[pl.ds(0, pad_p), :] = jnp.zeros((pad_p, D_MODEL), _F32)

        body(x_ref, o_ref, newhist_p_ref, ext_p_ref, *params, rows=rows, stride=1, pad=pad_p,
             first_step=j * rows, start_pos=0)

    @pl.when(i == n_prompt_blocks)
    def _():
        ext_s_ref[pl.ds(0, pad_s), :] = hist_ref[...]
        body(x_ref, o_ref, newhist_s_ref, ext_s_ref, *params, rows=rows, stride=stride_s,
             pad=pad_s, first_step=0, start_pos=PAST_LEN)


def _mixer_sublayer(body, name, x_all, hist_s, h_idx, params, *, n_seq, seq_len, n_sample,
                    stride_s, pad_p, pad_s):
    rows = MIX_ROWS
    assert seq_len % rows == 0 and n_sample == rows and hist_s.shape[1] == pad_s
    blocks_per_seq = seq_len // rows
    npb = n_seq * blocks_per_seq
    row_spec = pl.BlockSpec((rows, D_MODEL), lambda i: (i, 0))
    kernel = functools.partial(
        _mixer_kernel, body=body, n_params=len(params), rows=rows, blocks_per_seq=blocks_per_seq,
        n_prompt_blocks=npb, pad_p=pad_p, pad_s=pad_s, stride_s=stride_s)
    return pl.pallas_call(
        kernel,
        grid=(npb + 1,),
        in_specs=[row_spec, _stacked(hist_s, h_idx)] + [_stacked(a, k) for a, k in params],
        out_specs=[row_spec,
                   pl.BlockSpec((1, pad_p, D_MODEL),
                                lambda i: (jnp.minimum(i, npb - 1) // blocks_per_seq, 0, 0)),
                   pl.BlockSpec((pad_s, D_MODEL), lambda i: (0, 0))],
        out_shape=[jax.ShapeDtypeStruct(x_all.shape, _F32),
                   jax.ShapeDtypeStruct((n_seq, pad_p, D_MODEL), _F32),
                   jax.ShapeDtypeStruct((pad_s, D_MODEL), _F32)],
        scratch_shapes=[pltpu.VMEM((pad_p + rows, D_MODEL), _F32),
                        pltpu.VMEM((pad_s + rows, D_MODEL), _F32)],
        input_output_aliases={0: 0},
        compiler_params=pltpu.CompilerParams(
            dimension_semantics=("arbitrary",), vmem_limit_bytes=MIXER_VMEM_LIMIT_BYTES),
        name=name,
    )(x_all, hist_s, *[a for a, _ in params])


def kernel(x_prompt, x_sample, state_conv, state_pool, norm_gains, ffn_w_gate, ffn_w_up,
           ffn_w_down, conv_w_in, conv_kernel, conv_w_out, pool_w_group, pool_scale):
    batch, seq, _ = x_prompt.shape
    dec_batch, dec_seq, _ = x_sample.shape
    n_prompt = batch * seq
    n_sample = dec_batch * dec_seq
    n_conv = conv_w_in.shape[0]
    n_pool = pool_w_group.shape[0]

    xp_rows = x_prompt.reshape(n_prompt, D_MODEL)
    xs_rows = jnp.swapaxes(x_sample, 0, 1).reshape(n_sample, D_MODEL)

    gains = norm_gains.reshape(DEPTH * N_NORMS, 1, D_MODEL)
    ffn_w = (ffn_w_gate.reshape(2 * DEPTH, D_MODEL, D_FF),
             ffn_w_up.reshape(2 * DEPTH, D_MODEL, D_FF),
             ffn_w_down.reshape(2 * DEPTH, D_FF, D_MODEL))
    scale = pool_scale.reshape(n_pool, 1, D_MODEL)

    conv_pad_s = (CONV_W - 1) * dec_batch
    pool_pad_s = POOL_HIST * dec_batch
    conv_hist_s = jnp.swapaxes(state_conv, 1, 2).reshape(n_conv, conv_pad_s, D_MODEL)
    pool_hist_s = jnp.swapaxes(state_pool, 1, 2).reshape(n_pool, pool_pad_s, D_MODEL)
    ffn = functools.partial(_ffn_sublayer, n_prompt=n_prompt, n_sample=n_sample)
    weights = [(w, 0) for w in ffn_w]
    ffn_chunk_rows = (CAST_ROWS_IN, CAST_ROWS_IN, CAST_ROWS_DOWN)

    def ffn_cast_jobs(idx):
        return [(w, idx, r) for w, r in zip(ffn_w, ffn_chunk_rows)]
    mixer = functools.partial(_mixer_sublayer, n_seq=batch, seq_len=seq, n_sample=n_sample,
                              stride_s=dec_batch)

    new_conv_p, new_conv_s, new_pool_p, new_pool_s = [], [], [], []
    for i in range(DEPTH):
        g0 = i * N_NORMS
        xs = (xp_rows, xs_rows) if i == 0 else (x_all,)
        j = i // 2
        jobs = ffn_cast_jobs(2 * i + 1)
        if i % 2 == 0:
            jobs += [(conv_w_in, j, CAST_ROWS_IN), (conv_w_out, j, CAST_ROWS_IN)]
        (x_all,), cast = ffn(xs, gains, g0, weights, jobs, mode="first" if i == 0 else "mid")
        weights, conv16 = [(w, None) for w in cast[:3]], cast[3:]
        norms = [(gains, g0 + 2), (gains, g0 + 3)]
        if i % 2 == 0:
            x_all, hp, hs = mixer(
                _conv_body, "conv_mixer", x_all, conv_hist_s, j,
                norms + [(conv16[0], None), (conv_kernel, j), (conv16[1], None)],
                pad_p=CONV_PAD_ROWS, pad_s=conv_pad_s)
            new_conv_p.append(hp[:, CONV_PAD_ROWS - (CONV_W - 1):, :])
            new_conv_s.append(jnp.swapaxes(hs.reshape(CONV_W - 1, dec_batch, D_MODEL), 0, 1))
        else:
            x_all, hp, hs = mixer(
                _pool_body, "pool_mixer", x_all, pool_hist_s, j,
                norms + [(pool_w_group, j), (scale, j)],
                pad_p=POOL_PAD_ROWS, pad_s=pool_pad_s)
            new_pool_p.append(hp[:, POOL_PAD_ROWS - POOL_HIST:, :])
            new_pool_s.append(jnp.swapaxes(hs.reshape(POOL_HIST, dec_batch, D_MODEL), 0, 1))
        if i == DEPTH - 1:
            (yp_rows, ys_rows), _ = ffn((x_all,), gains, g0 + 4, weights, [], mode="last")
        else:
            (x_all,), cast = ffn((x_all,), gains, g0 + 4, weights, ffn_cast_jobs(2 * i + 2),
                                 mode="mid")
            weights = [(w, None) for w in cast]

    y_prompt = yp_rows.reshape(batch, seq, D_MODEL)
    y_sample = jnp.swapaxes(ys_rows.reshape(dec_seq, dec_batch, D_MODEL), 0, 1)
    return (y_prompt, y_sample, jnp.stack(new_conv_p), jnp.stack(new_conv_s),
            jnp.stack(new_pool_p), jnp.stack(new_pool_s))
```

```python
import functools

import jax
import jax.numpy as jnp
from jax import lax
from jax.experimental import pallas as pl
from jax.experimental.pallas import tpu as pltpu

D_MODEL = 1024
D_FF = 2816
DEPTH = 4
N_NORMS = 6
CONV_W = 3
POOL_WINDOWS = (2, 4, 8, 16)
POOL_GROUP = D_MODEL // len(POOL_WINDOWS)
POOL_HIST = max(POOL_WINDOWS) - 1
PAST_LEN = 16384
EPS = 1e-6

SUBLANES = 8
FFN_ROWS = 512
FFN_SUB_ROWS = 256
CAST_ROWS_IN = 32
CAST_ROWS_DOWN = 176
MIX_ROWS = 512
CONV_PAD_ROWS = SUBLANES
POOL_PAD_ROWS = 2 * SUBLANES
FFN_VMEM_LIMIT_BYTES = 36 * 1024 * 1024
FFN_CAST_VMEM_LIMIT_BYTES = 60 * 1024 * 1024
FFN_F32_VMEM_LIMIT_BYTES = 60 * 1024 * 1024
MIXER_VMEM_LIMIT_BYTES = 30 * 1024 * 1024

_F32 = jnp.float32
_BF16 = jnp.bfloat16


def _rmsnorm(x, g):
    ms = jnp.mean(x * x, axis=-1, keepdims=True)
    return x * lax.rsqrt(ms + EPS) * g


def _dot(a, b):
    return jnp.dot(a, b, preferred_element_type=_F32)


def _whole(arr):
    nd = arr.ndim
    return pl.BlockSpec(arr.shape, lambda *_: (0,) * nd, pipeline_mode=pl.Buffered(1))


def _stacked(arr, idx):
    if idx is None:
        return _whole(arr)
    tail = arr.shape[1:]
    return pl.BlockSpec((None,) + tail, lambda *_: (idx,) + (0,) * len(tail),
                        pipeline_mode=pl.Buffered(1))


def _cast_side_job(i, last, jobs, dst_hbm, stage, ostage, sem_in, sem_out):
    slot = i & 1
    n_mats = len(jobs)
    chunk_rows = [st.shape[1] for st in stage]
    n_chunks = [dst.shape[0] // r for dst, r in zip(dst_hbm, chunk_rows)]
    by_count = {}
    for k, n in enumerate(n_chunks):
        by_count.setdefault(n, []).append(k)

    def rows_of(k, c):
        r = chunk_rows[k]
        return pl.ds(pl.multiple_of(c * r, r), r)

    def in_copy(k, c, s):
        src, idx = jobs[k]
        return pltpu.make_async_copy(src.at[idx, rows_of(k, c)], stage[k].at[s], sem_in.at[s, k])

    def out_copy(k, c, s):
        return pltpu.make_async_copy(ostage[k].at[s], dst_hbm[k].at[rows_of(k, c)],
                                     sem_out.at[s, k])

    def each(ks, fn):
        def run():
            for k in ks:
                fn(k)
        return run

    def prologue():
        pl.when(i == 0)(each(range(n_mats), lambda k: in_copy(k, 0, 0).start()))
        for n, ks in by_count.items():
            pl.when(i < n)(each(ks, lambda k: in_copy(k, i, slot).wait()))
            pl.when(i + 1 < n)(each(ks, lambda k: in_copy(k, i + 1, 1 - slot).start()))
            pl.when((i >= 2) & (i - 2 < n))(each(ks, lambda k: out_copy(k, i - 2, slot).wait()))

    def cast():
        for k in range(n_mats):
            ostage[k].at[slot][...] = stage[k].at[slot][...].astype(_BF16)

    def epilogue():
        for n, ks in by_count.items():
            pl.when(i < n)(each(ks, lambda k: out_copy(k, i, slot).start()))
            for c in range(max(last - 1, 0), n):
                assert c <= last
                pl.when(i == last)(each(ks, lambda k, c=c: out_copy(k, c, c & 1).wait()))

    return prologue, cast, epilogue


def _ffn_kernel(*refs, mode, n_prompt_blocks, n_steps, cast_idx):
    n_x = 2 if mode == "first" else 1
    n_out = 2 if mode == "last" else 1
    n_cast = len(cast_idx)
    x_refs, refs = refs[:n_x], refs[n_x:]
    (gpre_ref, gpost_ref, wg_ref, wu_ref, wd_ref), refs = refs[:5], refs[5:]
    src_hbm, refs = refs[:n_cast], refs[n_cast:]
    out_refs, refs = refs[:n_out], refs[n_out:]
    i = pl.program_id(0)
    is_prompt = i < n_prompt_blocks
    op_dtype = wg_ref.dtype
    sub_rows = FFN_SUB_ROWS if op_dtype == _BF16 else FFN_ROWS

    if n_cast:
        dst_hbm, stage, ostage = refs[:n_cast], refs[n_cast:2 * n_cast], refs[2 * n_cast:3 * n_cast]
        sem_in, sem_out = refs[3 * n_cast:]
        prologue, cast, epilogue = _cast_side_job(
            i, n_steps - 1, list(zip(src_hbm, cast_idx)), dst_hbm, stage, ostage, sem_in, sem_out)
        prologue()
        cast()

    for s in range(FFN_ROWS // sub_rows):
        rows = pl.ds(s * sub_rows, sub_rows)
        if mode == "first":
            x = jnp.where(is_prompt, x_refs[0][rows, :], x_refs[1][rows, :])
        else:
            x = x_refs[0][rows, :]
        h = _rmsnorm(x, gpre_ref[...]).astype(op_dtype)
        a = _dot(h, wg_ref[...])
        b = _dot(h, wu_ref[...])
        g = (a * jax.nn.sigmoid(a) * b).astype(op_dtype)
        y = _dot(g, wd_ref[...])
        out = x + _rmsnorm(y, 0.5 * gpost_ref[...])
        out_refs[0][rows, :] = out

    if mode == "last":
        @pl.when(i == 0)
        def _():
            out_refs[1][...] = out_refs[0][...]

    if n_cast:
        epilogue()


def _ffn_sublayer(xs, gains, g_idx, weights, cast_jobs, *, mode, n_prompt, n_sample):
    n = n_prompt + n_sample
    rows = FFN_ROWS
    assert n_prompt % rows == 0 and n_sample == rows
    n_steps = n // rows
    npb = n_prompt // rows
    row_spec = pl.BlockSpec((rows, D_MODEL), lambda i: (i, 0))
    prompt_spec = pl.BlockSpec((rows, D_MODEL), lambda i: (jnp.minimum(i, npb - 1), 0))
    sample_spec = pl.BlockSpec((rows, D_MODEL), lambda i: (0, 0))
    any_spec = pl.BlockSpec(memory_space=pl.ANY)
    x_all_shape = jax.ShapeDtypeStruct((n, D_MODEL), _F32)
    if mode == "first":
        x_specs, out_specs, out_shape, aliases = [prompt_spec, sample_spec], [row_spec], [x_all_shape], {}
    elif mode == "last":
        x_specs, aliases = [row_spec], {}
        out_specs = [prompt_spec, sample_spec]
        out_shape = [jax.ShapeDtypeStruct((n_prompt, D_MODEL), _F32),
                     jax.ShapeDtypeStruct((n_sample, D_MODEL), _F32)]
    else:
        x_specs, out_specs, out_shape, aliases = [row_spec], [row_spec], [x_all_shape], {0: 0}
    if mode == "last":
        x_specs = [pl.BlockSpec((rows, D_MODEL), lambda i: (jnp.where(i == 0, npb, i - 1), 0))]
        out_specs[0] = pl.BlockSpec((rows, D_MODEL), lambda i: (jnp.maximum(i - 1, 0), 0))
    n_main_out = len(out_specs)
    n_cast = len(cast_jobs)
    if weights[0][0].dtype == _F32:
        vmem_limit = FFN_F32_VMEM_LIMIT_BYTES
    else:
        vmem_limit = FFN_CAST_VMEM_LIMIT_BYTES if n_cast else FFN_VMEM_LIMIT_BYTES
    scratch = []
    if n_cast:
        for w, _, r in cast_jobs:
            assert w.shape[1] % r == 0 and w.shape[1] // r <= n_steps
        out_specs = out_specs + [any_spec] * n_cast
        out_shape = out_shape + [jax.ShapeDtypeStruct(w.shape[1:], _BF16) for w, _, _ in cast_jobs]
        scratch = ([pltpu.VMEM((2, r, w.shape[2]), _F32) for w, _, r in cast_jobs]
                   + [pltpu.VMEM((2, r, w.shape[2]), _BF16) for w, _, r in cast_jobs]
                   + [pltpu.SemaphoreType.DMA((2, n_cast)), pltpu.SemaphoreType.DMA((2, n_cast))])
    outs = pl.pallas_call(
        functools.partial(_ffn_kernel, mode=mode, n_prompt_blocks=npb, n_steps=n_steps,
                          cast_idx=tuple(idx for _, idx, _ in cast_jobs)),
        grid=(n_steps,),
        in_specs=x_specs + [_stacked(gains, g_idx), _stacked(gains, g_idx + 1)]
        + [_stacked(w, k) for w, k in weights] + [any_spec] * n_cast,
        out_specs=out_specs,
        out_shape=out_shape,
        scratch_shapes=scratch,
        input_output_aliases=aliases,
        compiler_params=pltpu.CompilerParams(
            dimension_semantics=("arbitrary",), vmem_limit_bytes=vmem_limit),
        name="ffn_" + mode,
    )(*xs, gains, gains, *[w for w, _ in weights], *[w for w, _, _ in cast_jobs])
    return outs[:n_main_out], list(outs[n_main_out:])


def _conv_body(x_ref, o_ref, newhist_ref, ext_ref, gpre_ref, gpost_ref, win_ref, ck_ref,
               wout_ref, *, rows, stride, pad, first_step, start_pos):
    del first_step, start_pos
    x = x_ref[...]
    h = _rmsnorm(x, gpre_ref[...]).astype(_BF16)
    gate_c = _dot(h, win_ref[:, D_MODEL:2 * D_MODEL])
    v = _dot(h, win_ref[:, 2 * D_MODEL:])
    z = gate_c * v
    ext_ref[pl.ds(pad, rows), :] = z
    ck = ck_ref[...]
    conv = (ck[0:1] * ext_ref[pl.ds(pad - 2 * stride, rows), :]
            + ck[1:2] * ext_ref[pl.ds(pad - stride, rows), :]
            + ck[2:3] * z)
    gate_b = _dot(h, win_ref[:, :D_MODEL])
    y = _dot((gate_b * conv).astype(_BF16), wout_ref[...])
    o_ref[...] = x + _rmsnorm(y, gpost_ref[...])

    tail = ext_ref[pl.ds(rows, pad), :]
    newhist_ref[...] = tail.reshape(newhist_ref.shape)
    ext_ref[pl.ds(0, pad), :] = tail


def _pool_body(x_ref, o_ref, newhist_ref, ext_ref, gpre_ref, gpost_ref, wgrp_ref, scale_ref,
               *, rows, stride, pad, first_step, start_pos):
    x = x_ref[...]
    u = _rmsnorm(x, gpre_ref[...])
    ext_ref[pl.ds(pad, rows), :] = u

    step = lax.broadcasted_iota(jnp.int32, (rows, 1), 0) // stride
    pos = step + first_step + start_pos

    ys = []
    for g, w in enumerate(POOL_WINDOWS):
        cols = slice(g * POOL_GROUP, (g + 1) * POOL_GROUP)
        acc = ext_ref[:, cols]
        span = 1
        while span < w:
            acc = acc + pltpu.roll(acc, span * stride, axis=0)
            span *= 2
        win_sum = acc[pad:, :]
        count = jnp.minimum(pos + 1, w).astype(_F32)
        diff = win_sum / count - u[:, cols]
        ys.append(_dot(diff, wgrp_ref[g]))
    y = jnp.concatenate(ys, axis=-1) * scale_ref[...]
    o_ref[...] = x + _rmsnorm(y, gpost_ref[...])

    tail = ext_ref[pl.ds(rows, pad), :]
    newhist_ref[...] = tail.reshape(newhist_ref.shape)
    ext_ref[pl.ds(0, pad), :] = tail


def _mixer_kernel(x_ref, hist_ref, *refs, body, n_params, rows, blocks_per_seq,
                  n_prompt_blocks, pad_p, pad_s, stride_s):
    params = refs[:n_params]
    o_ref, newhist_p_ref, newhist_s_ref, ext_p_ref, ext_s_ref = refs[n_params:]
    i = pl.program_id(0)

    @pl.when(i < n_prompt_blocks)
    def _():
        j = lax.rem(i, jnp.int32(blocks_per_seq))

        @pl.when(j == 0)
        def _():
            ext_p_ref[pl.ds(0, pad_p), :] = jnp.zeros((pad_p, D_MODEL), _F32)

        body(x_ref, o_ref, newhist_p_ref, ext_p_ref, *params, rows=rows, stride=1, pad=pad_p,
             first_step=j * rows, start_pos=0)

    @pl.when(i == n_prompt_blocks)
    def _():
        ext_s_ref[pl.ds(0, pad_s), :] = hist_ref[...]
        body(x_ref, o_ref, newhist_s_ref, ext_s_ref, *params, rows=rows, stride=stride_s,
             pad=pad_s, first_step=0, start_pos=PAST_LEN)


def _mixer_sublayer(body, name, x_all, hist_s, h_idx, params, *, n_seq, seq_len, n_sample,
                    stride_s, pad_p, pad_s):
    rows = MIX_ROWS
    assert seq_len % rows == 0 and n_sample == rows and hist_s.shape[1] == pad_s
    blocks_per_seq = seq_len // rows
    npb = n_seq * blocks_per_seq
    row_spec = pl.BlockSpec((rows, D_MODEL), lambda i: (i, 0))
    kernel = functools.partial(
        _mixer_kernel, body=body, n_params=len(params), rows=rows, blocks_per_seq=blocks_per_seq,
        n_prompt_blocks=npb, pad_p=pad_p, pad_s=pad_s, stride_s=stride_s)
    return pl.pallas_call(
        kernel,
        grid=(npb + 1,),
        in_specs=[row_spec, _stacked(hist_s, h_idx)] + [_stacked(a, k) for a, k in params],
        out_specs=[row_spec,
                   pl.BlockSpec((1, pad_p, D_MODEL),
                                lambda i: (jnp.minimum(i, npb - 1) // blocks_per_seq, 0, 0)),
                   pl.BlockSpec((pad_s, D_MODEL), lambda i: (0, 0))],
        out_shape=[jax.ShapeDtypeStruct(x_all.shape, _F32),
                   jax.ShapeDtypeStruct((n_seq, pad_p, D_MODEL), _F32),
                   jax.ShapeDtypeStruct((pad_s, D_MODEL), _F32)],
        scratch_shapes=[pltpu.VMEM((pad_p + rows, D_MODEL), _F32),
                        pltpu.VMEM((pad_s + rows, D_MODEL), _F32)],
        input_output_aliases={0: 0},
        compiler_params=pltpu.CompilerParams(
            dimension_semantics=("arbitrary",), vmem_limit_bytes=MIXER_VMEM_LIMIT_BYTES),
        name=name,
    )(x_all, hist_s, *[a for a, _ in params])


def kernel(x_prompt, x_sample, state_conv, state_pool, norm_gains, ffn_w_gate, ffn_w_up,
           ffn_w_down, conv_w_in, conv_kernel, conv_w_out, pool_w_group, pool_scale):
    batch, seq, _ = x_prompt.shape
    dec_batch, dec_seq, _ = x_sample.shape
    n_prompt = batch * seq
    n_sample = dec_batch * dec_seq
    n_conv = conv_w_in.shape[0]
    n_pool = pool_w_group.shape[0]

    xp_rows = x_prompt.reshape(n_prompt, D_MODEL)
    xs_rows = jnp.swapaxes(x_sample, 0, 1).reshape(n_sample, D_MODEL)

    gains = norm_gains.reshape(DEPTH * N_NORMS, 1, D_MODEL)
    ffn_w = (ffn_w_gate.reshape(2 * DEPTH, D_MODEL, D_FF),
             ffn_w_up.reshape(2 * DEPTH, D_MODEL, D_FF),
             ffn_w_down.reshape(2 * DEPTH, D_FF, D_MODEL))
    scale = pool_scale.reshape(n_pool, 1, D_MODEL)

    conv_pad_s = (CONV_W - 1) * dec_batch
    pool_pad_s = POOL_HIST * dec_batch
    conv_hist_s = jnp.swapaxes(state_conv, 1, 2).reshape(n_conv, conv_pad_s, D_MODEL)
    pool_hist_s = jnp.swapaxes(state_pool, 1, 2).reshape(n_pool, pool_pad_s, D_MODEL)
    ffn = functools.partial(_ffn_sublayer, n_prompt=n_prompt, n_sample=n_sample)
    weights = [(w, 0) for w in ffn_w]
    ffn_chunk_rows = (CAST_ROWS_IN, CAST_ROWS_IN, CAST_ROWS_DOWN)

    def ffn_cast_jobs(idx):
        return [(w, idx, r) for w, r in zip(ffn_w, ffn_chunk_rows)]
    mixer = functools.partial(_mixer_sublayer, n_seq=batch, seq_len=seq, n_sample=n_sample,
                              stride_s=dec_batch)

    new_conv_p, new_conv_s, new_pool_p, new_pool_s = [], [], [], []
    for i in range(DEPTH):
        g0 = i * N_NORMS
        xs = (xp_rows, xs_rows) if i == 0 else (x_all,)
        j = i // 2
        jobs = ffn_cast_jobs(2 * i + 1)
        if i % 2 == 0:
            jobs += [(conv_w_in, j, CAST_ROWS_IN), (conv_w_out, j, CAST_ROWS_IN)]
        (x_all,), cast = ffn(xs, gains, g0, weights, jobs, mode="first" if i == 0 else "mid")
        weights, conv16 = [(w, None) for w in cast[:3]], cast[3:]
        norms = [(gains, g0 + 2), (gains, g0 + 3)]
        if i % 2 == 0:
            x_all, hp, hs = mixer(
                _conv_body, "conv_mixer", x_all, conv_hist_s, j,
                norms + [(conv16[0], None), (conv_kernel, j), (conv16[1], None)],
                pad_p=CONV_PAD_ROWS, pad_s=conv_pad_s)
            new_conv_p.append(hp[:, CONV_PAD_ROWS - (CONV_W - 1):, :])
            new_conv_s.append(jnp.swapaxes(hs.reshape(CONV_W - 1, dec_batch, D_MODEL), 0, 1))
        else:
            x_all, hp, hs = mixer(
                _pool_body, "pool_mixer", x_all, pool_hist_s, j,
                norms + [(pool_w_group, j), (scale, j)],
                pad_p=POOL_PAD_ROWS, pad_s=pool_pad_s)
            new_pool_p.append(hp[:, POOL_PAD_ROWS - POOL_HIST:, :])
            new_pool_s.append(jnp.swapaxes(hs.reshape(POOL_HIST, dec_batch, D_MODEL), 0, 1))
        if i == DEPTH - 1:
            (yp_rows, ys_rows), _ = ffn((x_all,), gains, g0 + 4, weights, [], mode="last")
        else:
            (x_all,), cast = ffn((x_all,), gains, g0 + 4, weights, ffn_cast_jobs(2 * i + 2),
                                 mode="mid")
            weights = [(w, None) for w in cast]

    y_prompt = yp_rows.reshape(batch, seq, D_MODEL)
    y_sample = jnp.swapaxes(ys_rows.reshape(dec_seq, dec_batch, D_MODEL), 0, 1)
    return (y_prompt, y_sample, jnp.stack(new_conv_p), jnp.stack(new_conv_s),
            jnp.stack(new_pool_p), jnp.stack(new_pool_s))
```

```python
import functools

import jax
import jax.numpy as jnp
from jax import lax
from jax.experimental import pallas as pl
from jax.experimental.pallas import tpu as pltpu

D_MODEL = 1024
D_FF = 2816
DEPTH = 4
N_NORMS = 6
CONV_W = 3
POOL_WINDOWS = (2, 4, 8, 16)
POOL_GROUP = D_MODEL // len(POOL_WINDOWS)
POOL_HIST = max(POOL_WINDOWS) - 1
PAST_LEN = 16384
EPS = 1e-6

SUBLANES = 8
FFN_ROWS = 512
FFN_ROWS_MID = 768
FFN_SUB_ROWS = 256
MIX_ROWS = 512
CONV_PAD_ROWS = SUBLANES
POOL_PAD_ROWS = 2 * SUBLANES
FFN_VMEM_LIMIT_BYTES = 40 * 1024 * 1024
FFN_CAST_VMEM_LIMIT_BYTES = 56 * 1024 * 1024
FFN_F32_VMEM_LIMIT_BYTES = 60 * 1024 * 1024
MIXER_VMEM_LIMIT_BYTES = 32 * 1024 * 1024

_F32 = jnp.float32
_BF16 = jnp.bfloat16


def _rmsnorm(x, g):
    ms = jnp.mean(x * x, axis=-1, keepdims=True)
    return x * lax.rsqrt(ms + EPS) * g


def _dot(a, b):
    return jnp.dot(a, b, preferred_element_type=_F32)


def _whole(arr):
    nd = arr.ndim
    return pl.BlockSpec(arr.shape, lambda *_: (0,) * nd, pipeline_mode=pl.Buffered(1))


def _stacked(arr, idx):
    if idx is None:
        return _whole(arr)
    tail = arr.shape[1:]
    return pl.BlockSpec((None,) + tail, lambda *_: (idx,) + (0,) * len(tail),
                        pipeline_mode=pl.Buffered(1))


def _cast_side_job(i, last, jobs, dst_hbm, stage, ostage, sem_in, sem_out):
    slot = i & 1
    n_mats = len(jobs)
    chunk_rows = [st.shape[1] for st in stage]
    n_chunks = [dst.shape[0] // r for dst, r in zip(dst_hbm, chunk_rows)]
    by_count = {}
    for k, n in enumerate(n_chunks):
        by_count.setdefault(n, []).append(k)

    def rows_of(k, c):
        r = chunk_rows[k]
        return pl.ds(pl.multiple_of(c * r, r), r)

    def in_copy(k, c, s):
        src, idx = jobs[k]
        return pltpu.make_async_copy(src.at[idx, rows_of(k, c)], stage[k].at[s], sem_in.at[s, k])

    def out_copy(k, c, s):
        return pltpu.make_async_copy(ostage[k].at[s], dst_hbm[k].at[rows_of(k, c)],
                                     sem_out.at[s, k])

    def each(ks, fn):
        def run():
            for k in ks:
                fn(k)
        return run

    def prologue():
        pl.when(i == 0)(each(range(n_mats), lambda k: in_copy(k, 0, 0).start()))
        for n, ks in by_count.items():
            pl.when(i < n)(each(ks, lambda k: in_copy(k, i, slot).wait()))
            pl.when(i + 1 < n)(each(ks, lambda k: in_copy(k, i + 1, 1 - slot).start()))
            pl.when((i >= 2) & (i - 2 < n))(each(ks, lambda k: out_copy(k, i - 2, slot).wait()))

    def cast():
        for k in range(n_mats):
            ostage[k].at[slot][...] = stage[k].at[slot][...].astype(_BF16)

    def epilogue():
        for n, ks in by_count.items():
            pl.when(i < n)(each(ks, lambda k: out_copy(k, i, slot).start()))
            for c in range(max(last - 1, 0), n):
                assert c <= last
                pl.when(i == last)(each(ks, lambda k, c=c: out_copy(k, c, c & 1).wait()))

    return prologue, cast, epilogue


def _ffn_kernel(*refs, mode, n_prompt_blocks, n_steps, cast_idx):
    n_x = 2 if mode == "first" else 1
    n_out = 2 if mode == "last" else 1
    n_cast = len(cast_idx)
    x_refs, refs = refs[:n_x], refs[n_x:]
    (gpre_ref, gpost_ref, wg_ref, wu_ref, wd_ref), refs = refs[:5], refs[5:]
    src_hbm, refs = refs[:n_cast], refs[n_cast:]
    out_refs, refs = refs[:n_out], refs[n_out:]
    i = pl.program_id(0)
    is_prompt = i < n_prompt_blocks
    op_dtype = wg_ref.dtype
    block_rows = x_refs[0].shape[0]
    sub_rows = FFN_SUB_ROWS if op_dtype == _BF16 else block_rows

    if n_cast:
        dst_hbm, stage, ostage = refs[:n_cast], refs[n_cast:2 * n_cast], refs[2 * n_cast:3 * n_cast]
        sem_in, sem_out = refs[3 * n_cast:]
        prologue, cast, epilogue = _cast_side_job(
            i, n_steps - 1, list(zip(src_hbm, cast_idx)), dst_hbm, stage, ostage, sem_in, sem_out)
        prologue()
        cast()

    for s in range(block_rows // sub_rows):
        rows = pl.ds(s * sub_rows, sub_rows)
        if mode == "first":
            x = jnp.where(is_prompt, x_refs[0][rows, :], x_refs[1][rows, :])
        else:
            x = x_refs[0][rows, :]
        h = _rmsnorm(x, gpre_ref[...]).astype(op_dtype)
        a = _dot(h, wg_ref[...])
        b = _dot(h, wu_ref[...])
        g = (a * jax.nn.sigmoid(a) * b).astype(op_dtype)
        y = _dot(g, wd_ref[...])
        out = x + _rmsnorm(y, 0.5 * gpost_ref[...])
        out_refs[0][rows, :] = out

    if mode == "last":
        @pl.when(i == 0)
        def _():
            out_refs[1][...] = out_refs[0][...]

    if n_cast:
        epilogue()


def _ffn_sublayer(xs, gains, g_idx, weights, cast_jobs, *, mode, n_prompt, n_sample):
    def chunk_rows(n_rows, n_steps):
        return next(r for r in range(16, n_rows + 1, 16)
                    if n_rows % r == 0 and n_rows // r <= n_steps)

    n = n_prompt + n_sample
    rows = FFN_ROWS_MID if mode == "mid" else FFN_ROWS
    assert n % rows == 0 and (mode == "mid" or (n_prompt % rows == 0 and n_sample == rows))
    n_steps = n // rows
    npb = n_prompt // rows
    row_spec = pl.BlockSpec((rows, D_MODEL), lambda i: (i, 0))
    prompt_spec = pl.BlockSpec((rows, D_MODEL), lambda i: (jnp.minimum(i, npb - 1), 0))
    sample_spec = pl.BlockSpec((rows, D_MODEL), lambda i: (0, 0))
    any_spec = pl.BlockSpec(memory_space=pl.ANY)
    x_all_shape = jax.ShapeDtypeStruct((n, D_MODEL), _F32)
    if mode == "first":
        x_specs, out_specs, out_shape, aliases = [prompt_spec, sample_spec], [row_spec], [x_all_shape], {}
    elif mode == "last":
        x_specs, aliases = [row_spec], {}
        out_specs = [prompt_spec, sample_spec]
        out_shape = [jax.ShapeDtypeStruct((n_prompt, D_MODEL), _F32),
                     jax.ShapeDtypeStruct((n_sample, D_MODEL), _F32)]
    else:
        x_specs, out_specs, out_shape, aliases = [row_spec], [row_spec], [x_all_shape], {0: 0}
    if mode == "last":
        x_specs = [pl.BlockSpec((rows, D_MODEL), lambda i: (jnp.where(i == 0, npb, i - 1), 0))]
        out_specs[0] = pl.BlockSpec((rows, D_MODEL), lambda i: (jnp.maximum(i - 1, 0), 0))
    n_main_out = len(out_specs)
    n_cast = len(cast_jobs)
    if weights[0][0].dtype == _F32:
        vmem_limit = FFN_F32_VMEM_LIMIT_BYTES
    else:
        vmem_limit = FFN_CAST_VMEM_LIMIT_BYTES if n_cast else FFN_VMEM_LIMIT_BYTES
    scratch = []
    if n_cast:
        cast_jobs = [(w, idx, chunk_rows(w.shape[1], n_steps)) for w, idx in cast_jobs]
        out_specs = out_specs + [any_spec] * n_cast
        out_shape = out_shape + [jax.ShapeDtypeStruct(w.shape[1:], _BF16) for w, _, _ in cast_jobs]
        scratch = ([pltpu.VMEM((2, r, w.shape[2]), _F32) for w, _, r in cast_jobs]
                   + [pltpu.VMEM((2, r, w.shape[2]), _BF16) for w, _, r in cast_jobs]
                   + [pltpu.SemaphoreType.DMA((2, n_cast)), pltpu.SemaphoreType.DMA((2, n_cast))])
    outs = pl.pallas_call(
        functools.partial(_ffn_kernel, mode=mode, n_prompt_blocks=npb, n_steps=n_steps,
                          cast_idx=tuple(job[1] for job in cast_jobs)),
        grid=(n_steps,),
        in_specs=x_specs + [_stacked(gains, g_idx), _stacked(gains, g_idx + 1)]
        + [_stacked(w, k) for w, k in weights] + [any_spec] * n_cast,
        out_specs=out_specs,
        out_shape=out_shape,
        scratch_shapes=scratch,
        input_output_aliases=aliases,
        compiler_params=pltpu.CompilerParams(
            dimension_semantics=("arbitrary",), vmem_limit_bytes=vmem_limit),
        name="ffn_" + mode,
    )(*xs, gains, gains, *[w for w, _ in weights], *[job[0] for job in cast_jobs])
    return outs[:n_main_out], list(outs[n_main_out:])


def _conv_body(x_ref, o_ref, newhist_ref, ext_ref, gpre_ref, gpost_ref, win_ref, ck_ref,
               wout_ref, *, rows, stride, pad, first_step, start_pos):
    del first_step, start_pos
    x = x_ref[...]
    h = _rmsnorm(x, gpre_ref[...]).astype(_BF16)
    gate_c = _dot(h, win_ref[:, D_MODEL:2 * D_MODEL])
    v = _dot(h, win_ref[:, 2 * D_MODEL:])
    z = gate_c * v
    ext_ref[pl.ds(pad, rows), :] = z
    ck = ck_ref[...]
    conv = (ck[0:1] * ext_ref[pl.ds(pad - 2 * stride, rows), :]
            + ck[1:2] * ext_ref[pl.ds(pad - stride, rows), :]
            + ck[2:3] * z)
    gate_b = _dot(h, win_ref[:, :D_MODEL])
    y = _dot((gate_b * conv).astype(_BF16), wout_ref[...])
    o_ref[...] = x + _rmsnorm(y, gpost_ref[...])

    tail = ext_ref[pl.ds(rows, pad), :]
    newhist_ref[...] = tail.reshape(newhist_ref.shape)
    ext_ref[pl.ds(0, pad), :] = tail


def _pool_body(x_ref, o_ref, newhist_ref, ext_ref, gpre_ref, gpost_ref, wgrp_ref, scale_ref,
               *, rows, stride, pad, first_step, start_pos):
    x = x_ref[...]
    u = _rmsnorm(x, gpre_ref[...])
    ext_ref[pl.ds(pad, rows), :] = u

    step = lax.broadcasted_iota(jnp.int32, (rows, 1), 0) // stride
    pos = step + first_step + start_pos

    ys = []
    for g, w in enumerate(POOL_WINDOWS):
        cols = slice(g * POOL_GROUP, (g + 1) * POOL_GROUP)
        acc = ext_ref[:, cols]
        span = 1
        while span < w:
            acc = acc + pltpu.roll(acc, span * stride, axis=0)
            span *= 2
        win_sum = acc[pad:, :]
        count = jnp.minimum(pos + 1, w).astype(_F32)
        diff = win_sum / count - u[:, cols]
        ys.append(_dot(diff, wgrp_ref[g]))
    y = jnp.concatenate(ys, axis=-1) * scale_ref[...]
    o_ref[...] = x + _rmsnorm(y, gpost_ref[...])

    tail = ext_ref[pl.ds(rows, pad), :]
    newhist_ref[...] = tail.reshape(newhist_ref.shape)
    ext_ref[pl.ds(0, pad), :] = tail


def _mixer_kernel(x_ref, hist_ref, *refs, body, n_params, rows, blocks_per_seq,
                  n_prompt_blocks, pad_p, pad_s, stride_s):
    params = refs[:n_params]
    o_ref, newhist_p_ref, newhist_s_ref, ext_p_ref, ext_s_ref = refs[n_params:]
    i = pl.program_id(0)

    @pl.when(i < n_prompt_blocks)
    def _():
        j = lax.rem(i, jnp.int32(blocks_per_seq))

        @pl.when(j == 0)
        def _():
            ext_p_ref[pl.ds(0, pad_p), :] = jnp.zeros((pad_p, D_MODEL), _F32)

        body(x_ref, o_ref, newhist_p_ref, ext_p_ref, *params, rows=rows, stride=1, pad=pad_p,
             first_step=j * rows, start_pos=0)

    @pl.when(i == n_prompt_blocks)
    def _():
        ext_s_ref[pl.ds(0, pad_s), :] = hist_ref[...]
        body(x_ref, o_ref, newhist_s_ref, ext_s_ref, *params, rows=rows, stride=stride_s,
             pad=pad_s, first_step=0, start_pos=PAST_LEN)


def _mixer_sublayer(body, name, x_all, hist_s, h_idx, params, *, n_seq, seq_len, n_sample,
                    stride_s, pad_p, pad_s):
    rows = MIX_ROWS
    assert seq_len % rows == 0 and n_sample == rows and hist_s.shape[1] == pad_s
    blocks_per_seq = seq_len // rows
    npb = n_seq * blocks_per_seq
    row_spec = pl.BlockSpec((rows, D_MODEL), lambda i: (i, 0))
    kernel = functools.partial(
        _mixer_kernel, body=body, n_params=len(params), rows=rows, blocks_per_seq=blocks_per_seq,
        n_prompt_blocks=npb, pad_p=pad_p, pad_s=pad_s, stride_s=stride_s)
    return pl.pallas_call(
        kernel,
        grid=(npb + 1,),
        in_specs=[row_spec, _stacked(hist_s, h_idx)] + [_stacked(a, k) for a, k in params],
        out_specs=[row_spec,
                   pl.BlockSpec((1, pad_p, D_MODEL),
                                lambda i: (jnp.minimum(i, npb - 1) // blocks_per_seq, 0, 0)),
                   pl.BlockSpec((pad_s, D_MODEL), lambda i: (0, 0))],
        out_shape=[jax.ShapeDtypeStruct(x_all.shape, _F32),
                   jax.ShapeDtypeStruct((n_seq, pad_p, D_MODEL), _F32),
                   jax.ShapeDtypeStruct((pad_s, D_MODEL), _F32)],
        scratch_shapes=[pltpu.VMEM((pad_p + rows, D_MODEL), _F32),
                        pltpu.VMEM((pad_s + rows, D_MODEL), _F32)],
        input_output_aliases={0: 0},
        compiler_params=pltpu.CompilerParams(
            dimension_semantics=("arbitrary",), vmem_limit_bytes=MIXER_VMEM_LIMIT_BYTES),
        name=name,
    )(x_all, hist_s, *[a for a, _ in params])


def kernel(x_prompt, x_sample, state_conv, state_pool, norm_gains, ffn_w_gate, ffn_w_up,
           ffn_w_down, conv_w_in, conv_kernel, conv_w_out, pool_w_group, pool_scale):
    batch, seq, _ = x_prompt.shape
    dec_batch, dec_seq, _ = x_sample.shape
    n_prompt = batch * seq
    n_sample = dec_batch * dec_seq
    n_conv = conv_w_in.shape[0]
    n_pool = pool_w_group.shape[0]

    xp_rows = x_prompt.reshape(n_prompt, D_MODEL)
    xs_rows = jnp.swapaxes(x_sample, 0, 1).reshape(n_sample, D_MODEL)

    gains = norm_gains.reshape(DEPTH * N_NORMS, 1, D_MODEL)
    ffn_w = (ffn_w_gate.reshape(2 * DEPTH, D_MODEL, D_FF),
             ffn_w_up.reshape(2 * DEPTH, D_MODEL, D_FF),
             ffn_w_down.reshape(2 * DEPTH, D_FF, D_MODEL))
    scale = pool_scale.reshape(n_pool, 1, D_MODEL)

    conv_pad_s = (CONV_W - 1) * dec_batch
    pool_pad_s = POOL_HIST * dec_batch
    conv_hist_s = jnp.swapaxes(state_conv, 1, 2).reshape(n_conv, conv_pad_s, D_MODEL)
    pool_hist_s = jnp.swapaxes(state_pool, 1, 2).reshape(n_pool, pool_pad_s, D_MODEL)
    ffn = functools.partial(_ffn_sublayer, n_prompt=n_prompt, n_sample=n_sample)
    weights = [(w, 0) for w in ffn_w]

    def ffn_cast_jobs(idx):
        return [(w, idx) for w in ffn_w]
    mixer = functools.partial(_mixer_sublayer, n_seq=batch, seq_len=seq, n_sample=n_sample,
                              stride_s=dec_batch)

    new_conv_p, new_conv_s, new_pool_p, new_pool_s = [], [], [], []
    for i in range(DEPTH):
        g0 = i * N_NORMS
        xs = (xp_rows, xs_rows) if i == 0 else (x_all,)
        j = i // 2
        jobs = ffn_cast_jobs(2 * i + 1)
        if i % 2 == 0:
            jobs += [(conv_w_in, j), (conv_w_out, j)]
        (x_all,), cast = ffn(xs, gains, g0, weights, jobs, mode="first" if i == 0 else "mid")
        weights, conv16 = [(w, None) for w in cast[:3]], cast[3:]
        norms = [(gains, g0 + 2), (gains, g0 + 3)]
        if i % 2 == 0:
            x_all, hp, hs = mixer(
                _conv_body, "conv_mixer", x_all, conv_hist_s, j,
                norms + [(conv16[0], None), (conv_kernel, j), (conv16[1], None)],
                pad_p=CONV_PAD_ROWS, pad_s=conv_pad_s)
            new_conv_p.append(hp[:, CONV_PAD_ROWS - (CONV_W - 1):, :])
            new_conv_s.append(jnp.swapaxes(hs.reshape(CONV_W - 1, dec_batch, D_MODEL), 0, 1))
        else:
            x_all, hp, hs = mixer(
                _pool_body, "pool_mixer", x_all, pool_hist_s, j,
                norms + [(pool_w_group, j), (scale, j)],
                pad_p=POOL_PAD_ROWS, pad_s=pool_pad_s)
            new_pool_p.append(hp[:, POOL_PAD_ROWS - POOL_HIST:, :])
            new_pool_s.append(jnp.swapaxes(hs.reshape(POOL_HIST, dec_batch, D_MODEL), 0, 1))
        if i == DEPTH - 1:
            (yp_rows, ys_rows), _ = ffn((x_all,), gains, g0 + 4, weights, [], mode="last")
        else:
            (x_all,), cast = ffn((x_all,), gains, g0 + 4, weights, ffn_cast_jobs(2 * i + 2),
                                 mode="mid")
            weights = [(w, None) for w in cast]

    y_prompt = yp_rows.reshape(batch, seq, D_MODEL)
    y_sample = jnp.swapaxes(ys_rows.reshape(dec_seq, dec_batch, D_MODEL), 0, 1)
    return (y_prompt, y_sample, jnp.stack(new_conv_p), jnp.stack(new_conv_s),
            jnp.stack(new_pool_p), jnp.stack(new_pool_s))
```

```python
import functools

import jax
import jax.numpy as jnp
from jax import lax
from jax.experimental import pallas as pl
from jax.experimental.pallas import tpu as pltpu

D_MODEL = 1024
D_FF = 2816
DEPTH = 4
N_NORMS = 6
CONV_W = 3
POOL_WINDOWS = (2, 4, 8, 16)
POOL_GROUP = D_MODEL // len(POOL_WINDOWS)
POOL_HIST = max(POOL_WINDOWS) - 1
PAST_LEN = 16384
EPS = 1e-6

SUBLANES = 8
FFN_ROWS = 512
FFN_ROWS_MID = 768
FFN_SUB_BLOCKS = 2
MIX_ROWS = 512
CONV_PAD_ROWS = SUBLANES
POOL_PAD_ROWS = 2 * SUBLANES
FFN_VMEM_LIMIT_BYTES = 40 * 1024 * 1024
FFN_CAST_VMEM_LIMIT_BYTES = 56 * 1024 * 1024
FFN_F32_VMEM_LIMIT_BYTES = 60 * 1024 * 1024
MIXER_VMEM_LIMIT_BYTES = 32 * 1024 * 1024

_F32 = jnp.float32
_BF16 = jnp.bfloat16


def _rmsnorm(x, g):
    ms = jnp.mean(x * x, axis=-1, keepdims=True)
    return x * lax.rsqrt(ms + EPS) * g


def _dot(a, b):
    return jnp.dot(a, b, preferred_element_type=_F32)


def _whole(arr):
    nd = arr.ndim
    return pl.BlockSpec(arr.shape, lambda *_: (0,) * nd, pipeline_mode=pl.Buffered(1))


def _stacked(arr, idx):
    if idx is None:
        return _whole(arr)
    tail = arr.shape[1:]
    return pl.BlockSpec((None,) + tail, lambda *_: (idx,) + (0,) * len(tail),
                        pipeline_mode=pl.Buffered(1))


def _cast_side_job(i, last, jobs, dst_hbm, stage, ostage, sem_in, sem_out):
    slot = i & 1
    n_mats = len(jobs)
    chunk_rows = [st.shape[1] for st in stage]
    n_chunks = [dst.shape[0] // r for dst, r in zip(dst_hbm, chunk_rows)]
    by_count = {}
    for k, n in enumerate(n_chunks):
        by_count.setdefault(n, []).append(k)

    def rows_of(k, c):
        r = chunk_rows[k]
        return pl.ds(pl.multiple_of(c * r, r), r)

    def in_copy(k, c, s):
        src, idx = jobs[k]
        return pltpu.make_async_copy(src.at[idx, rows_of(k, c)], stage[k].at[s], sem_in.at[s, k])

    def out_copy(k, c, s):
        return pltpu.make_async_copy(ostage[k].at[s], dst_hbm[k].at[rows_of(k, c)],
                                     sem_out.at[s, k])

    def each(ks, fn):
        def run():
            for k in ks:
                fn(k)
        return run

    def prologue():
        pl.when(i == 0)(each(range(n_mats), lambda k: in_copy(k, 0, 0).start()))
        for n, ks in by_count.items():
            pl.when(i < n)(each(ks, lambda k: in_copy(k, i, slot).wait()))
            pl.when(i + 1 < n)(each(ks, lambda k: in_copy(k, i + 1, 1 - slot).start()))
            pl.when((i >= 2) & (i - 2 < n))(each(ks, lambda k: out_copy(k, i - 2, slot).wait()))

    def cast():
        for k in range(n_mats):
            ostage[k].at[slot][...] = stage[k].at[slot][...].astype(_BF16)

    def epilogue():
        for n, ks in by_count.items():
            pl.when(i < n)(each(ks, lambda k: out_copy(k, i, slot).start()))
            for c in range(max(last - 1, 0), n):
                assert c <= last
                pl.when(i == last)(each(ks, lambda k, c=c: out_copy(k, c, c & 1).wait()))

    return prologue, cast, epilogue


def _ffn_kernel(*refs, mode, n_prompt_blocks, n_steps, cast_idx):
    n_x = 2 if mode == "first" else 1
    n_out = 2 if mode == "last" else 1
    n_cast = len(cast_idx)
    x_refs, refs = refs[:n_x], refs[n_x:]
    (gpre_ref, gpost_ref, wg_ref, wu_ref, wd_ref), refs = refs[:5], refs[5:]
    src_hbm, refs = refs[:n_cast], refs[n_cast:]
    out_refs, refs = refs[:n_out], refs[n_out:]
    i = pl.program_id(0)
    is_prompt = i < n_prompt_blocks
    op_dtype = wg_ref.dtype
    block_rows = x_refs[0].shape[0]
    sub_rows = block_rows // FFN_SUB_BLOCKS if op_dtype == _BF16 else block_rows

    if n_cast:
        dst_hbm, stage, ostage = refs[:n_cast], refs[n_cast:2 * n_cast], refs[2 * n_cast:3 * n_cast]
        sem_in, sem_out = refs[3 * n_cast:]
        prologue, cast, epilogue = _cast_side_job(
            i, n_steps - 1, list(zip(src_hbm, cast_idx)), dst_hbm, stage, ostage, sem_in, sem_out)
        prologue()
        cast()

    for s in range(block_rows // sub_rows):
        rows = pl.ds(s * sub_rows, sub_rows)
        if mode == "first":
            x = jnp.where(is_prompt, x_refs[0][rows, :], x_refs[1][rows, :])
        else:
            x = x_refs[0][rows, :]
        h = _rmsnorm(x, gpre_ref[...]).astype(op_dtype)
        a = _dot(h, wg_ref[...])
        b = _dot(h, wu_ref[...])
        g = (a * jax.nn.sigmoid(a) * b).astype(op_dtype)
        y = _dot(g, wd_ref[...])
        out = x + _rmsnorm(y, 0.5 * gpost_ref[...])
        out_refs[0][rows, :] = out

    if mode == "last":
        @pl.when(i == 0)
        def _():
            out_refs[1][...] = out_refs[0][...]

    if n_cast:
        epilogue()


def _ffn_sublayer(xs, gains, g_idx, weights, cast_jobs, *, mode, n_prompt, n_sample):
    def chunk_rows(n_rows, n_steps):
        return next(r for r in range(16, n_rows + 1, 16)
                    if n_rows % r == 0 and n_rows // r <= n_steps)

    n = n_prompt + n_sample
    rows = FFN_ROWS_MID if mode == "mid" else FFN_ROWS
    assert n % rows == 0 and (mode == "mid" or (n_prompt % rows == 0 and n_sample == rows))
    n_steps = n // rows
    npb = n_prompt // rows
    row_spec = pl.BlockSpec((rows, D_MODEL), lambda i: (i, 0))
    prompt_spec = pl.BlockSpec((rows, D_MODEL), lambda i: (jnp.minimum(i, npb - 1), 0))
    sample_spec = pl.BlockSpec((rows, D_MODEL), lambda i: (0, 0))
    any_spec = pl.BlockSpec(memory_space=pl.ANY)
    x_all_shape = jax.ShapeDtypeStruct((n, D_MODEL), _F32)
    if mode == "first":
        x_specs, out_specs, out_shape, aliases = [prompt_spec, sample_spec], [row_spec], [x_all_shape], {}
    elif mode == "last":
        x_specs, aliases = [row_spec], {}
        out_specs = [prompt_spec, sample_spec]
        out_shape = [jax.ShapeDtypeStruct((n_prompt, D_MODEL), _F32),
                     jax.ShapeDtypeStruct((n_sample, D_MODEL), _F32)]
    else:
        x_specs, out_specs, out_shape, aliases = [row_spec], [row_spec], [x_all_shape], {0: 0}
    if mode == "last":
        x_specs = [pl.BlockSpec((rows, D_MODEL), lambda i: (jnp.where(i == 0, npb, i - 1), 0))]
        out_specs[0] = pl.BlockSpec((rows, D_MODEL), lambda i: (jnp.maximum(i - 1, 0), 0))
    n_main_out = len(out_specs)
    n_cast = len(cast_jobs)
    if weights[0][0].dtype == _F32:
        vmem_limit = FFN_F32_VMEM_LIMIT_BYTES
    else:
        vmem_limit = FFN_CAST_VMEM_LIMIT_BYTES if n_cast else FFN_VMEM_LIMIT_BYTES
    scratch = []
    if n_cast:
        cast_jobs = [(w, idx, chunk_rows(w.shape[1], n_steps)) for w, idx in cast_jobs]
        out_specs = out_specs + [any_spec] * n_cast
        out_shape = out_shape + [jax.ShapeDtypeStruct(w.shape[1:], _BF16) for w, _, _ in cast_jobs]
        scratch = ([pltpu.VMEM((2, r, w.shape[2]), _F32) for w, _, r in cast_jobs]
                   + [pltpu.VMEM((2, r, w.shape[2]), _BF16) for w, _, r in cast_jobs]
                   + [pltpu.SemaphoreType.DMA((2, n_cast)), pltpu.SemaphoreType.DMA((2, n_cast))])
    outs = pl.pallas_call(
        functools.partial(_ffn_kernel, mode=mode, n_prompt_blocks=npb, n_steps=n_steps,
                          cast_idx=tuple(job[1] for job in cast_jobs)),
        grid=(n_steps,),
        in_specs=x_specs + [_stacked(gains, g_idx), _stacked(gains, g_idx + 1)]
        + [_stacked(w, k) for w, k in weights] + [any_spec] * n_cast,
        out_specs=out_specs,
        out_shape=out_shape,
        scratch_shapes=scratch,
        input_output_aliases=aliases,
        compiler_params=pltpu.CompilerParams(
            dimension_semantics=("arbitrary",), vmem_limit_bytes=vmem_limit),
        name="ffn_" + mode,
    )(*xs, gains, gains, *[w for w, _ in weights], *[job[0] for job in cast_jobs])
    return outs[:n_main_out], list(outs[n_main_out:])


def _conv_body(x_ref, o_ref, newhist_ref, ext_ref, gpre_ref, gpost_ref, win_ref, ck_ref,
               wout_ref, *, rows, stride, pad, first_step, start_pos):
    del first_step, start_pos
    x = x_ref[...]
    h = _rmsnorm(x, gpre_ref[...]).astype(_BF16)
    gate_c = _dot(h, win_ref[:, D_MODEL:2 * D_MODEL])
    v = _dot(h, win_ref[:, 2 * D_MODEL:])
    z = gate_c * v
    ext_ref[pl.ds(pad, rows), :] = z
    ck = ck_ref[...]
    conv = (ck[0:1] * ext_ref[pl.ds(pad - 2 * stride, rows), :]
            + ck[1:2] * ext_ref[pl.ds(pad - stride, rows), :]
            + ck[2:3] * z)
    gate_b = _dot(h, win_ref[:, :D_MODEL])
    y = _dot((gate_b * conv).astype(_BF16), wout_ref[...])
    o_ref[...] = x + _rmsnorm(y, gpost_ref[...])

    tail = ext_ref[pl.ds(rows, pad), :]
    newhist_ref[...] = tail.reshape(newhist_ref.shape)
    ext_ref[pl.ds(0, pad), :] = tail


def _pool_body(x_ref, o_ref, newhist_ref, ext_ref, gpre_ref, gpost_ref, wgrp_ref, scale_ref,
               *, rows, stride, pad, first_step, start_pos):
    x = x_ref[...]
    u = _rmsnorm(x, gpre_ref[...])
    ext_ref[pl.ds(pad, rows), :] = u

    step = lax.broadcasted_iota(jnp.int32, (rows, 1), 0) // stride
    pos = step + first_step + start_pos

    ys = []
    for g, w in enumerate(POOL_WINDOWS):
        cols = slice(g * POOL_GROUP, (g + 1) * POOL_GROUP)
        acc = ext_ref[:, cols]
        span = 1
        while span < w:
            acc = acc + pltpu.roll(acc, span * stride, axis=0)
            span *= 2
        win_sum = acc[pad:, :]
        count = jnp.minimum(pos + 1, w).astype(_F32)
        diff = win_sum / count - u[:, cols]
        ys.append(_dot(diff, wgrp_ref[g]))
    y = jnp.concatenate(ys, axis=-1) * scale_ref[...]
    o_ref[...] = x + _rmsnorm(y, gpost_ref[...])

    tail = ext_ref[pl.ds(rows, pad), :]
    newhist_ref[...] = tail.reshape(newhist_ref.shape)
    ext_ref[pl.ds(0, pad), :] = tail


def _mixer_kernel(x_ref, hist_ref, *refs, body, n_params, rows, blocks_per_seq,
                  n_prompt_blocks, pad_p, pad_s, stride_s):
    params = refs[:n_params]
    o_ref, newhist_p_ref, newhist_s_ref, ext_p_ref, ext_s_ref = refs[n_params:]
    i = pl.program_id(0)

    @pl.when(i < n_prompt_blocks)
    def _():
        j = lax.rem(i, jnp.int32(blocks_per_seq))

        @pl.when(j == 0)
        def _():
            ext_p_ref[pl.ds(0, pad_p), :] = jnp.zeros((pad_p, D_MODEL), _F32)

        body(x_ref, o_ref, newhist_p_ref, ext_p_ref, *params, rows=rows, stride=1, pad=pad_p,
             first_step=j * rows, start_pos=0)

    @pl.when(i == n_prompt_blocks)
    def _():
        ext_s_ref[pl.ds(0, pad_s), :] = hist_ref[...]
        body(x_ref, o_ref, newhist_s_ref, ext_s_ref, *params, rows=rows, stride=stride_s,
             pad=pad_s, first_step=0, start_pos=PAST_LEN)


def _mixer_sublayer(body, name, x_all, hist_s, h_idx, params, *, n_seq, seq_len, n_sample,
                    stride_s, pad_p, pad_s):
    rows = MIX_ROWS
    assert seq_len % rows == 0 and n_sample == rows and hist_s.shape[1] == pad_s
    blocks_per_seq = seq_len // rows
    npb = n_seq * blocks_per_seq
    row_spec = pl.BlockSpec((rows, D_MODEL), lambda i: (i, 0))
    kernel = functools.partial(
        _mixer_kernel, body=body, n_params=len(params), rows=rows, blocks_per_seq=blocks_per_seq,
        n_prompt_blocks=npb, pad_p=pad_p, pad_s=pad_s, stride_s=stride_s)
    return pl.pallas_call(
        kernel,
        grid=(npb + 1,),
        in_specs=[row_spec, _stacked(hist_s, h_idx)] + [_stacked(a, k) for a, k in params],
        out_specs=[row_spec,
                   pl.BlockSpec((1, pad_p, D_MODEL),
                                lambda i: (jnp.minimum(i, npb - 1) // blocks_per_seq, 0, 0)),
                   pl.BlockSpec((pad_s, D_MODEL), lambda i: (0, 0))],
        out_shape=[jax.ShapeDtypeStruct(x_all.shape, _F32),
                   jax.ShapeDtypeStruct((n_seq, pad_p, D_MODEL), _F32),
                   jax.ShapeDtypeStruct((pad_s, D_MODEL), _F32)],
        scratch_shapes=[pltpu.VMEM((pad_p + rows, D_MODEL), _F32),
                        pltpu.VMEM((pad_s + rows, D_MODEL), _F32)],
        input_output_aliases={0: 0},
        compiler_params=pltpu.CompilerParams(
            dimension_semantics=("arbitrary",), vmem_limit_bytes=MIXER_VMEM_LIMIT_BYTES),
        name=name,
    )(x_all, hist_s, *[a for a, _ in params])


def kernel(x_prompt, x_sample, state_conv, state_pool, norm_gains, ffn_w_gate, ffn_w_up,
           ffn_w_down, conv_w_in, conv_kernel, conv_w_out, pool_w_group, pool_scale):
    batch, seq, _ = x_prompt.shape
    dec_batch, dec_seq, _ = x_sample.shape
    n_prompt = batch * seq
    n_sample = dec_batch * dec_seq
    n_conv = conv_w_in.shape[0]
    n_pool = pool_w_group.shape[0]

    xp_rows = x_prompt.reshape(n_prompt, D_MODEL)
    xs_rows = jnp.swapaxes(x_sample, 0, 1).reshape(n_sample, D_MODEL)

    gains = norm_gains.reshape(DEPTH * N_NORMS, 1, D_MODEL)
    ffn_w = (ffn_w_gate.reshape(2 * DEPTH, D_MODEL, D_FF),
             ffn_w_up.reshape(2 * DEPTH, D_MODEL, D_FF),
             ffn_w_down.reshape(2 * DEPTH, D_FF, D_MODEL))
    scale = pool_scale.reshape(n_pool, 1, D_MODEL)

    conv_pad_s = (CONV_W - 1) * dec_batch
    pool_pad_s = POOL_HIST * dec_batch
    conv_hist_s = jnp.swapaxes(state_conv, 1, 2).reshape(n_conv, conv_pad_s, D_MODEL)
    pool_hist_s = jnp.swapaxes(state_pool, 1, 2).reshape(n_pool, pool_pad_s, D_MODEL)
    ffn = functools.partial(_ffn_sublayer, n_prompt=n_prompt, n_sample=n_sample)
    weights = [(w, 0) for w in ffn_w]

    def ffn_cast_jobs(idx):
        return [(w, idx) for w in ffn_w]
    mixer = functools.partial(_mixer_sublayer, n_seq=batch, seq_len=seq, n_sample=n_sample,
                              stride_s=dec_batch)

    new_conv_p, new_conv_s, new_pool_p, new_pool_s = [], [], [], []
    for i in range(DEPTH):
        g0 = i * N_NORMS
        xs = (xp_rows, xs_rows) if i == 0 else (x_all,)
        j = i // 2
        jobs = ffn_cast_jobs(2 * i + 1)
        if i % 2 == 0:
            jobs += [(conv_w_in, j), (conv_w_out, j)]
        (x_all,), cast = ffn(xs, gains, g0, weights, jobs, mode="first" if i == 0 else "mid")
        weights, conv16 = [(w, None) for w in cast[:3]], cast[3:]
        norms = [(gains, g0 + 2), (gains, g0 + 3)]
        if i % 2 == 0:
            x_all, hp, hs = mixer(
                _conv_body, "conv_mixer", x_all, conv_hist_s, j,
                norms + [(conv16[0], None), (conv_kernel, j), (conv16[1], None)],
                pad_p=CONV_PAD_ROWS, pad_s=conv_pad_s)
            new_conv_p.append(hp[:, CONV_PAD_ROWS - (CONV_W - 1):, :])
            new_conv_s.append(jnp.swapaxes(hs.reshape(CONV_W - 1, dec_batch, D_MODEL), 0, 1))
        else:
            x_all, hp, hs = mixer(
                _pool_body, "pool_mixer", x_all, pool_hist_s, j,
                norms + [(pool_w_group, j), (scale, j)],
                pad_p=POOL_PAD_ROWS, pad_s=pool_pad_s)
            new_pool_p.append(hp[:, POOL_PAD_ROWS - POOL_HIST:, :])
            new_pool_s.append(jnp.swapaxes(hs.reshape(POOL_HIST, dec_batch, D_MODEL), 0, 1))
        if i == DEPTH - 1:
            (yp_rows, ys_rows), _ = ffn((x_all,), gains, g0 + 4, weights, [], mode="last")
        else:
            (x_all,), cast = ffn((x_all,), gains, g0 + 4, weights, ffn_cast_jobs(2 * i + 2),
                                 mode="mid")
            weights = [(w, None) for w in cast]

    y_prompt = yp_rows.reshape(batch, seq, D_MODEL)
    y_sample = jnp.swapaxes(ys_rows.reshape(dec_seq, dec_batch, D_MODEL), 0, 1)
    return (y_prompt, y_sample, jnp.stack(new_conv_p), jnp.stack(new_conv_s),
            jnp.stack(new_pool_p), jnp.stack(new_pool_s))
```

```python
import functools

import jax
import jax.numpy as jnp
from jax import lax
from jax.experimental import pallas as pl
from jax.experimental.pallas import tpu as pltpu

D_MODEL = 1024
D_FF = 2816
DEPTH = 4
N_NORMS = 6
CONV_W = 3
POOL_WINDOWS = (2, 4, 8, 16)
POOL_GROUP = D_MODEL // len(POOL_WINDOWS)
POOL_HIST = max(POOL_WINDOWS) - 1
PAST_LEN = 16384
EPS = 1e-6

SUBLANES = 8
FFN_ROWS = 512
FFN_ROWS_MID = 768
FFN_SUB_ROWS = 256
FF_TILE = 256
MIX_ROWS = 512
CONV_PAD_ROWS = SUBLANES
POOL_PAD_ROWS = 2 * SUBLANES
FFN_VMEM_LIMIT_BYTES = 40 * 1024 * 1024
FFN_CAST_VMEM_LIMIT_BYTES = 56 * 1024 * 1024
FFN_F32_VMEM_LIMIT_BYTES = 60 * 1024 * 1024
MIXER_VMEM_LIMIT_BYTES = 32 * 1024 * 1024

_F32 = jnp.float32
_BF16 = jnp.bfloat16


def _rmsnorm(x, g):
    ms = jnp.mean(x * x, axis=-1, keepdims=True)
    return x * lax.rsqrt(ms + EPS) * g


def _dot(a, b):
    return jnp.dot(a, b, preferred_element_type=_F32)


def _whole(arr):
    nd = arr.ndim
    return pl.BlockSpec(arr.shape, lambda *_: (0,) * nd, pipeline_mode=pl.Buffered(1))


def _stacked(arr, idx):
    if idx is None:
        return _whole(arr)
    tail = arr.shape[1:]
    return pl.BlockSpec((None,) + tail, lambda *_: (idx,) + (0,) * len(tail),
                        pipeline_mode=pl.Buffered(1))


def _cast_side_job(i, last, jobs, dst_hbm, stage, ostage, sem_in, sem_out):
    slot = i & 1
    n_mats = len(jobs)
    chunk_rows = [st.shape[1] for st in stage]
    n_chunks = [dst.shape[0] // r for dst, r in zip(dst_hbm, chunk_rows)]
    by_count = {}
    for k, n in enumerate(n_chunks):
        by_count.setdefault(n, []).append(k)

    def rows_of(k, c):
        r = chunk_rows[k]
        return pl.ds(pl.multiple_of(c * r, r), r)

    def in_copy(k, c, s):
        src, idx = jobs[k]
        return pltpu.make_async_copy(src.at[idx, rows_of(k, c)], stage[k].at[s], sem_in.at[s, k])

    def out_copy(k, c, s):
        return pltpu.make_async_copy(ostage[k].at[s], dst_hbm[k].at[rows_of(k, c)],
                                     sem_out.at[s, k])

    def each(ks, fn):
        def run():
            for k in ks:
                fn(k)
        return run

    def prologue():
        pl.when(i == 0)(each(range(n_mats), lambda k: in_copy(k, 0, 0).start()))
        for n, ks in by_count.items():
            pl.when(i < n)(each(ks, lambda k: in_copy(k, i, slot).wait()))
            pl.when(i + 1 < n)(each(ks, lambda k: in_copy(k, i + 1, 1 - slot).start()))
            pl.when((i >= 2) & (i - 2 < n))(each(ks, lambda k: out_copy(k, i - 2, slot).wait()))

    def cast():
        for k in range(n_mats):
            ostage[k].at[slot][...] = stage[k].at[slot][...].astype(_BF16)

    def epilogue():
        for n, ks in by_count.items():
            pl.when(i < n)(each(ks, lambda k: out_copy(k, i, slot).start()))
            for c in range(max(last - 1, 0), n):
                assert c <= last
                pl.when(i == last)(each(ks, lambda k, c=c: out_copy(k, c, c & 1).wait()))

    return prologue, cast, epilogue


def _ffn_kernel(*refs, mode, n_prompt_blocks, n_steps, cast_idx):
    n_x = 2 if mode == "first" else 1
    n_out = 2 if mode == "last" else 1
    n_cast = len(cast_idx)
    x_refs, refs = refs[:n_x], refs[n_x:]
    (gpre_ref, gpost_ref, wg_ref, wu_ref, wd_ref), refs = refs[:5], refs[5:]
    src_hbm, refs = refs[:n_cast], refs[n_cast:]
    out_refs, refs = refs[:n_out], refs[n_out:]
    i = pl.program_id(0)
    is_prompt = i < n_prompt_blocks
    op_dtype = wg_ref.dtype
    block_rows = x_refs[0].shape[0]
    sub_rows = FFN_SUB_ROWS if op_dtype == _BF16 else block_rows

    if n_cast:
        dst_hbm, stage, ostage = refs[:n_cast], refs[n_cast:2 * n_cast], refs[2 * n_cast:3 * n_cast]
        sem_in, sem_out = refs[3 * n_cast:]
        prologue, cast, epilogue = _cast_side_job(
            i, n_steps - 1, list(zip(src_hbm, cast_idx)), dst_hbm, stage, ostage, sem_in, sem_out)
        prologue()
        cast()

    for s in range(block_rows // sub_rows):
        rows = pl.ds(s * sub_rows, sub_rows)
        if mode == "first":
            x = jnp.where(is_prompt, x_refs[0][rows, :], x_refs[1][rows, :])
        else:
            x = x_refs[0][rows, :]
        h = _rmsnorm(x, gpre_ref[...]).astype(op_dtype)
        gs = []
        for c in range(0, D_FF, FF_TILE):
            a = _dot(h, wg_ref[:, c:c + FF_TILE])
            b = _dot(h, wu_ref[:, c:c + FF_TILE])
            gs.append((a * jax.nn.sigmoid(a) * b).astype(op_dtype))
        y = _dot(jnp.concatenate(gs, axis=-1), wd_ref[...])
        out = x + _rmsnorm(y, 0.5 * gpost_ref[...])
        out_refs[0][rows, :] = out

    if mode == "last":
        @pl.when(i == 0)
        def _():
            out_refs[1][...] = out_refs[0][...]

    if n_cast:
        epilogue()


def _ffn_sublayer(xs, gains, g_idx, weights, cast_jobs, *, mode, n_prompt, n_sample):
    def chunk_rows(n_rows, n_steps):
        return next(r for r in range(16, n_rows + 1, 16)
                    if n_rows % r == 0 and n_rows // r <= n_steps)

    n = n_prompt + n_sample
    rows = FFN_ROWS_MID if mode == "mid" else FFN_ROWS
    assert n % rows == 0 and (mode == "mid" or (n_prompt % rows == 0 and n_sample == rows))
    n_steps = n // rows
    npb = n_prompt // rows
    row_spec = pl.BlockSpec((rows, D_MODEL), lambda i: (i, 0))
    prompt_spec = pl.BlockSpec((rows, D_MODEL), lambda i: (jnp.minimum(i, npb - 1), 0))
    sample_spec = pl.BlockSpec((rows, D_MODEL), lambda i: (0, 0))
    any_spec = pl.BlockSpec(memory_space=pl.ANY)
    x_all_shape = jax.ShapeDtypeStruct((n, D_MODEL), _F32)
    if mode == "first":
        x_specs, out_specs, out_shape, aliases = [prompt_spec, sample_spec], [row_spec], [x_all_shape], {}
    elif mode == "last":
        x_specs, aliases = [row_spec], {}
        out_specs = [prompt_spec, sample_spec]
        out_shape = [jax.ShapeDtypeStruct((n_prompt, D_MODEL), _F32),
                     jax.ShapeDtypeStruct((n_sample, D_MODEL), _F32)]
    else:
        x_specs, out_specs, out_shape, aliases = [row_spec], [row_spec], [x_all_shape], {0: 0}
    if mode == "last":
        x_specs = [pl.BlockSpec((rows, D_MODEL), lambda i: (jnp.where(i == 0, npb, i - 1), 0))]
        out_specs[0] = pl.BlockSpec((rows, D_MODEL), lambda i: (jnp.maximum(i - 1, 0), 0))
    n_main_out = len(out_specs)
    n_cast = len(cast_jobs)
    if weights[0][0].dtype == _F32:
        vmem_limit = FFN_F32_VMEM_LIMIT_BYTES
    else:
        vmem_limit = FFN_CAST_VMEM_LIMIT_BYTES if n_cast else FFN_VMEM_LIMIT_BYTES
    scratch = []
    if n_cast:
        cast_jobs = [(w, idx, chunk_rows(w.shape[1], n_steps)) for w, idx in cast_jobs]
        out_specs = out_specs + [any_spec] * n_cast
        out_shape = out_shape + [jax.ShapeDtypeStruct(w.shape[1:], _BF16) for w, _, _ in cast_jobs]
        scratch = ([pltpu.VMEM((2, r, w.shape[2]), _F32) for w, _, r in cast_jobs]
                   + [pltpu.VMEM((2, r, w.shape[2]), _BF16) for w, _, r in cast_jobs]
                   + [pltpu.SemaphoreType.DMA((2, n_cast)), pltpu.SemaphoreType.DMA((2, n_cast))])
    outs = pl.pallas_call(
        functools.partial(_ffn_kernel, mode=mode, n_prompt_blocks=npb, n_steps=n_steps,
                          cast_idx=tuple(job[1] for job in cast_jobs)),
        grid=(n_steps,),
        in_specs=x_specs + [_stacked(gains, g_idx), _stacked(gains, g_idx + 1)]
        + [_stacked(w, k) for w, k in weights] + [any_spec] * n_cast,
        out_specs=out_specs,
        out_shape=out_shape,
        scratch_shapes=scratch,
        input_output_aliases=aliases,
        compiler_params=pltpu.CompilerParams(
            dimension_semantics=("arbitrary",), vmem_limit_bytes=vmem_limit),
        name="ffn_" + mode,
    )(*xs, gains, gains, *[w for w, _ in weights], *[job[0] for job in cast_jobs])
    return outs[:n_main_out], list(outs[n_main_out:])


def _conv_body(x_ref, o_ref, newhist_ref, ext_ref, gpre_ref, gpost_ref, win_ref, ck_ref,
               wout_ref, *, rows, stride, pad, first_step, start_pos):
    del first_step, start_pos
    x = x_ref[...]
    h = _rmsnorm(x, gpre_ref[...]).astype(_BF16)
    gate_c = _dot(h, win_ref[:, D_MODEL:2 * D_MODEL])
    v = _dot(h, win_ref[:, 2 * D_MODEL:])
    z = gate_c * v
    ext_ref[pl.ds(pad, rows), :] = z
    ck = ck_ref[...]
    conv = (ck[0:1] * ext_ref[pl.ds(pad - 2 * stride, rows), :]
            + ck[1:2] * ext_ref[pl.ds(pad - stride, rows), :]
            + ck[2:3] * z)
    gate_b = _dot(h, win_ref[:, :D_MODEL])
    y = _dot((gate_b * conv).astype(_BF16), wout_ref[...])
    o_ref[...] = x + _rmsnorm(y, gpost_ref[...])

    tail = ext_ref[pl.ds(rows, pad), :]
    newhist_ref[...] = tail.reshape(newhist_ref.shape)
    ext_ref[pl.ds(0, pad), :] = tail


def _pool_body(x_ref, o_ref, newhist_ref, ext_ref, gpre_ref, gpost_ref, wgrp_ref, scale_ref,
               *, rows, stride, pad, first_step, start_pos):
    x = x_ref[...]
    u = _rmsnorm(x, gpre_ref[...])
    ext_ref[pl.ds(pad, rows), :] = u

    step = lax.broadcasted_iota(jnp.int32, (rows, 1), 0) // stride
    pos = step + first_step + start_pos

    ys = []
    for g, w in enumerate(POOL_WINDOWS):
        cols = slice(g * POOL_GROUP, (g + 1) * POOL_GROUP)
        acc = ext_ref[:, cols]
        span = 1
        while span < w:
            acc = acc + pltpu.roll(acc, span * stride, axis=0)
            span *= 2
        win_sum = acc[pad:, :]
        count = jnp.minimum(pos + 1, w).astype(_F32)
        diff = win_sum / count - u[:, cols]
        ys.append(_dot(diff, wgrp_ref[g]))
    y = jnp.concatenate(ys, axis=-1) * scale_ref[...]
    o_ref[...] = x + _rmsnorm(y, gpost_ref[...])

    tail = ext_ref[pl.ds(rows, pad), :]
    newhist_ref[...] = tail.reshape(newhist_ref.shape)
    ext_ref[pl.ds(0, pad), :] = tail


def _mixer_kernel(x_ref, hist_ref, *refs, body, n_params, rows, blocks_per_seq,
                  n_prompt_blocks, pad_p, pad_s, stride_s):
    params = refs[:n_params]
    o_ref, newhist_p_ref, newhist_s_ref, ext_p_ref, ext_s_ref = refs[n_params:]
    i = pl.program_id(0)

    @pl.when(i < n_prompt_blocks)
    def _():
        j = lax.rem(i, jnp.int32(blocks_per_seq))

        @pl.when(j == 0)
        def _():
            ext_p_ref[pl.ds(0, pad_p), :] = jnp.zeros((pad_p, D_MODEL), _F32)

        body(x_ref, o_ref, newhist_p_ref, ext_p_ref, *params, rows=rows, stride=1, pad=pad_p,
             first_step=j * rows, start_pos=0)

    @pl.when(i == n_prompt_blocks)
    def _():
        ext_s_ref[pl.ds(0, pad_s), :] = hist_ref[...]
        body(x_ref, o_ref, newhist_s_ref, ext_s_ref, *params, rows=rows, stride=stride_s,
             pad=pad_s, first_step=0, start_pos=PAST_LEN)


def _mixer_sublayer(body, name, x_all, hist_s, h_idx, params, *, n_seq, seq_len, n_sample,
                    stride_s, pad_p, pad_s):
    rows = MIX_ROWS
    assert seq_len % rows == 0 and n_sample == rows and hist_s.shape[1] == pad_s
    blocks_per_seq = seq_len // rows
    npb = n_seq * blocks_per_seq
    row_spec = pl.BlockSpec((rows, D_MODEL), lambda i: (i, 0))
    kernel = functools.partial(
        _mixer_kernel, body=body, n_params=len(params), rows=rows, blocks_per_seq=blocks_per_seq,
        n_prompt_blocks=npb, pad_p=pad_p, pad_s=pad_s, stride_s=stride_s)
    return pl.pallas_call(
        kernel,
        grid=(npb + 1,),
        in_specs=[row_spec, _stacked(hist_s, h_idx)] + [_stacked(a, k) for a, k in params],
        out_specs=[row_spec,
                   pl.BlockSpec((1, pad_p, D_MODEL),
                                lambda i: (jnp.minimum(i, npb - 1) // blocks_per_seq, 0, 0)),
                   pl.BlockSpec((pad_s, D_MODEL), lambda i: (0, 0))],
        out_shape=[jax.ShapeDtypeStruct(x_all.shape, _F32),
                   jax.ShapeDtypeStruct((n_seq, pad_p, D_MODEL), _F32),
                   jax.ShapeDtypeStruct((pad_s, D_MODEL), _F32)],
        scratch_shapes=[pltpu.VMEM((pad_p + rows, D_MODEL), _F32),
                        pltpu.VMEM((pad_s + rows, D_MODEL), _F32)],
        input_output_aliases={0: 0},
        compiler_params=pltpu.CompilerParams(
            dimension_semantics=("arbitrary",), vmem_limit_bytes=MIXER_VMEM_LIMIT_BYTES),
        name=name,
    )(x_all, hist_s, *[a for a, _ in params])


def kernel(x_prompt, x_sample, state_conv, state_pool, norm_gains, ffn_w_gate, ffn_w_up,
           ffn_w_down, conv_w_in, conv_kernel, conv_w_out, pool_w_group, pool_scale):
    batch, seq, _ = x_prompt.shape
    dec_batch, dec_seq, _ = x_sample.shape
    n_prompt = batch * seq
    n_sample = dec_batch * dec_seq
    n_conv = conv_w_in.shape[0]
    n_pool = pool_w_group.shape[0]

    xp_rows = x_prompt.reshape(n_prompt, D_MODEL)
    xs_rows = jnp.swapaxes(x_sample, 0, 1).reshape(n_sample, D_MODEL)

    gains = norm_gains.reshape(DEPTH * N_NORMS, 1, D_MODEL)
    ffn_w = (ffn_w_gate.reshape(2 * DEPTH, D_MODEL, D_FF),
             ffn_w_up.reshape(2 * DEPTH, D_MODEL, D_FF),
             ffn_w_down.reshape(2 * DEPTH, D_FF, D_MODEL))
    scale = pool_scale.reshape(n_pool, 1, D_MODEL)

    conv_pad_s = (CONV_W - 1) * dec_batch
    pool_pad_s = POOL_HIST * dec_batch
    conv_hist_s = jnp.swapaxes(state_conv, 1, 2).reshape(n_conv, conv_pad_s, D_MODEL)
    pool_hist_s = jnp.swapaxes(state_pool, 1, 2).reshape(n_pool, pool_pad_s, D_MODEL)
    ffn = functools.partial(_ffn_sublayer, n_prompt=n_prompt, n_sample=n_sample)
    weights = [(w, 0) for w in ffn_w]

    def ffn_cast_jobs(idx):
        return [(w, idx) for w in ffn_w]
    mixer = functools.partial(_mixer_sublayer, n_seq=batch, seq_len=seq, n_sample=n_sample,
                              stride_s=dec_batch)

    new_conv_p, new_conv_s, new_pool_p, new_pool_s = [], [], [], []
    for i in range(DEPTH):
        g0 = i * N_NORMS
        xs = (xp_rows, xs_rows) if i == 0 else (x_all,)
        j = i // 2
        jobs = ffn_cast_jobs(2 * i + 1)
        if i % 2 == 0:
            jobs += [(conv_w_in, j), (conv_w_out, j)]
        (x_all,), cast = ffn(xs, gains, g0, weights, jobs, mode="first" if i == 0 else "mid")
        weights, conv16 = [(w, None) for w in cast[:3]], cast[3:]
        norms = [(gains, g0 + 2), (gains, g0 + 3)]
        if i % 2 == 0:
            x_all, hp, hs = mixer(
                _conv_body, "conv_mixer", x_all, conv_hist_s, j,
                norms + [(conv16[0], None), (conv_kernel, j), (conv16[1], None)],
                pad_p=CONV_PAD_ROWS, pad_s=conv_pad_s)
            new_conv_p.append(hp[:, CONV_PAD_ROWS - (CONV_W - 1):, :])
            new_conv_s.append(jnp.swapaxes(hs.reshape(CONV_W - 1, dec_batch, D_MODEL), 0, 1))
        else:
            x_all, hp, hs = mixer(
                _pool_body, "pool_mixer", x_all, pool_hist_s, j,
                norms + [(pool_w_group, j), (scale, j)],
                pad_p=POOL_PAD_ROWS, pad_s=pool_pad_s)
            new_pool_p.append(hp[:, POOL_PAD_ROWS - POOL_HIST:, :])
            new_pool_s.append(jnp.swapaxes(hs.reshape(POOL_HIST, dec_batch, D_MODEL), 0, 1))
        if i == DEPTH - 1:
            (yp_rows, ys_rows), _ = ffn((x_all,), gains, g0 + 4, weights, [], mode="last")
        else:
            (x_all,), cast = ffn((x_all,), gains, g0 + 4, weights, ffn_cast_jobs(2 * i + 2),
                                 mode="mid")
            weights = [(w, None) for w in cast]

    y_prompt = yp_rows.reshape(batch, seq, D_MODEL)
    y_sample = jnp.swapaxes(ys_rows.reshape(dec_seq, dec_batch, D_MODEL), 0, 1)
    return (y_prompt, y_sample, jnp.stack(new_conv_p), jnp.stack(new_conv_s),
            jnp.stack(new_pool_p), jnp.stack(new_pool_s))
```

```python
import functools

import jax
import jax.numpy as jnp
from jax import lax
from jax.experimental import pallas as pl
from jax.experimental.pallas import tpu as pltpu

D_MODEL = 1024
D_FF = 2816
DEPTH = 4
N_NORMS = 6
CONV_W = 3
POOL_WINDOWS = (2, 4, 8, 16)
POOL_GROUP = D_MODEL // len(POOL_WINDOWS)
POOL_HIST = max(POOL_WINDOWS) - 1
PAST_LEN = 16384
EPS = 1e-6

SUBLANES = 8
FFN_ROWS = 512
FFN_ROWS_MID = 768
FFN_SUB_ROWS = 256
FF_TILE = 256
MIX_ROWS = 512
CONV_PAD_ROWS = SUBLANES
POOL_PAD_ROWS = 2 * SUBLANES
FFN_VMEM_LIMIT_BYTES = 40 * 1024 * 1024
FFN_CAST_VMEM_LIMIT_BYTES = 56 * 1024 * 1024
FFN_F32_VMEM_LIMIT_BYTES = 60 * 1024 * 1024
MIXER_VMEM_LIMIT_BYTES = 32 * 1024 * 1024

_F32 = jnp.float32
_BF16 = jnp.bfloat16


def _rmsnorm(x, g):
    ms = jnp.mean(x * x, axis=-1, keepdims=True)
    return x * lax.rsqrt(ms + EPS) * g


def _dot(a, b):
    return jnp.dot(a, b, preferred_element_type=_F32)


def _whole(arr):
    nd = arr.ndim
    return pl.BlockSpec(arr.shape, lambda *_: (0,) * nd, pipeline_mode=pl.Buffered(1))


def _stacked(arr, idx):
    if idx is None:
        return _whole(arr)
    tail = arr.shape[1:]
    return pl.BlockSpec((None,) + tail, lambda *_: (idx,) + (0,) * len(tail),
                        pipeline_mode=pl.Buffered(1))


def _cast_side_job(i, last, jobs, dst_hbm, stage, ostage, sem_in, sem_out):
    slot = i & 1
    n_mats = len(jobs)
    chunk_rows = [st.shape[1] for st in stage]
    n_chunks = [dst.shape[0] // r for dst, r in zip(dst_hbm, chunk_rows)]
    by_count = {}
    for k, n in enumerate(n_chunks):
        by_count.setdefault(n, []).append(k)

    def rows_of(k, c):
        r = chunk_rows[k]
        return pl.ds(pl.multiple_of(c * r, r), r)

    def in_copy(k, c, s):
        src, idx = jobs[k]
        return pltpu.make_async_copy(src.at[idx, rows_of(k, c)], stage[k].at[s], sem_in.at[s, k])

    def out_copy(k, c, s):
        return pltpu.make_async_copy(ostage[k].at[s], dst_hbm[k].at[rows_of(k, c)],
                                     sem_out.at[s, k])

    def each(ks, fn):
        def run():
            for k in ks:
                fn(k)
        return run

    def prologue():
        pl.when(i == 0)(each(range(n_mats), lambda k: in_copy(k, 0, 0).start()))
        for n, ks in by_count.items():
            pl.when(i < n)(each(ks, lambda k: in_copy(k, i, slot).wait()))
            pl.when(i + 1 < n)(each(ks, lambda k: in_copy(k, i + 1, 1 - slot).start()))
            pl.when((i >= 2) & (i - 2 < n))(each(ks, lambda k: out_copy(k, i - 2, slot).wait()))

    def cast():
        for k in range(n_mats):
            ostage[k].at[slot][...] = stage[k].at[slot][...].astype(_BF16)

    def epilogue():
        for n, ks in by_count.items():
            pl.when(i < n)(each(ks, lambda k: out_copy(k, i, slot).start()))
            for c in range(max(last - 1, 0), n):
                assert c <= last
                pl.when(i == last)(each(ks, lambda k, c=c: out_copy(k, c, c & 1).wait()))

    return prologue, cast, epilogue


def _ffn_kernel(*refs, mode, n_prompt_blocks, n_steps, cast_idx):
    n_x = 2 if mode == "first" else 1
    n_out = 2 if mode == "last" else 1
    n_cast = len(cast_idx)
    x_refs, refs = refs[:n_x], refs[n_x:]
    (gpre_ref, gpost_ref, wg_ref, wu_ref, wd_ref), refs = refs[:5], refs[5:]
    src_hbm, refs = refs[:n_cast], refs[n_cast:]
    out_refs, refs = refs[:n_out], refs[n_out:]
    i = pl.program_id(0)
    is_prompt = i < n_prompt_blocks
    op_dtype = wg_ref.dtype
    block_rows = x_refs[0].shape[0]
    sub_rows = FFN_SUB_ROWS if op_dtype == _BF16 else block_rows

    if n_cast:
        dst_hbm, stage, ostage = refs[:n_cast], refs[n_cast:2 * n_cast], refs[2 * n_cast:3 * n_cast]
        sem_in, sem_out = refs[3 * n_cast:]
        prologue, cast, epilogue = _cast_side_job(
            i, n_steps - 1, list(zip(src_hbm, cast_idx)), dst_hbm, stage, ostage, sem_in, sem_out)
        prologue()
        cast()

    for s in range(block_rows // sub_rows):
        rows = pl.ds(s * sub_rows, sub_rows)
        if mode == "first":
            x = jnp.where(is_prompt, x_refs[0][rows, :], x_refs[1][rows, :])
        else:
            x = x_refs[0][rows, :]
        h = _rmsnorm(x, gpre_ref[...]).astype(op_dtype)
        gs = []
        for c in range(0, D_FF, FF_TILE):
            a = _dot(h, wg_ref[:, c:c + FF_TILE])
            b = _dot(h, wu_ref[:, c:c + FF_TILE])
            gs.append((a * jax.nn.sigmoid(a) * b).astype(op_dtype))
        y = _dot(jnp.concatenate(gs, axis=-1), wd_ref[...])
        out = x + _rmsnorm(y, 0.5 * gpost_ref[...])
        out_refs[0][rows, :] = out

    if mode == "last":
        @pl.when(i == 0)
        def _():
            out_refs[1][...] = out_refs[0][...]

    if n_cast:
        epilogue()


def _ffn_sublayer(xs, gains, g_idx, weights, cast_jobs, *, mode, n_prompt, n_sample):
    def chunk_rows(n_rows, n_steps):
        return next(r for r in range(16, n_rows + 1, 16)
                    if n_rows % r == 0 and n_rows // r <= n_steps)

    n = n_prompt + n_sample
    rows = FFN_ROWS_MID if mode == "mid" else FFN_ROWS
    assert n % rows == 0 and (mode == "mid" or (n_prompt % rows == 0 and n_sample == rows))
    n_steps = n // rows
    npb = n_prompt // rows
    row_spec = pl.BlockSpec((rows, D_MODEL), lambda i: (i, 0))
    prompt_spec = pl.BlockSpec((rows, D_MODEL), lambda i: (jnp.minimum(i, npb - 1), 0))
    sample_spec = pl.BlockSpec((rows, D_MODEL), lambda i: (0, 0))
    any_spec = pl.BlockSpec(memory_space=pl.ANY)
    x_all_shape = jax.ShapeDtypeStruct((n, D_MODEL), _F32)
    if mode == "first":
        x_specs, out_specs, out_shape, aliases = [prompt_spec, sample_spec], [row_spec], [x_all_shape], {}
    elif mode == "last":
        x_specs, aliases = [row_spec], {}
        out_specs = [prompt_spec, sample_spec]
        out_shape = [jax.ShapeDtypeStruct((n_prompt, D_MODEL), _F32),
                     jax.ShapeDtypeStruct((n_sample, D_MODEL), _F32)]
    else:
        x_specs, out_specs, out_shape, aliases = [row_spec], [row_spec], [x_all_shape], {0: 0}
    if mode == "last":
        x_specs = [pl.BlockSpec((rows, D_MODEL), lambda i: (jnp.where(i == 0, npb, i - 1), 0))]
        out_specs[0] = pl.BlockSpec((rows, D_MODEL), lambda i: (jnp.maximum(i - 1, 0), 0))
    n_main_out = len(out_specs)
    n_cast = len(cast_jobs)
    if weights[0][0].dtype == _F32:
        vmem_limit = FFN_F32_VMEM_LIMIT_BYTES
    else:
        vmem_limit = FFN_CAST_VMEM_LIMIT_BYTES if n_cast else FFN_VMEM_LIMIT_BYTES
    scratch = []
    if n_cast:
        cast_jobs = [(w, idx, chunk_rows(w.shape[1], n_steps)) for w, idx in cast_jobs]
        out_specs = out_specs + [any_spec] * n_cast
        out_shape = out_shape + [jax.ShapeDtypeStruct(w.shape[1:], _BF16) for w, _, _ in cast_jobs]
        scratch = ([pltpu.VMEM((2, r, w.shape[2]), _F32) for w, _, r in cast_jobs]
                   + [pltpu.VMEM((2, r, w.shape[2]), _BF16) for w, _, r in cast_jobs]
                   + [pltpu.SemaphoreType.DMA((2, n_cast)), pltpu.SemaphoreType.DMA((2, n_cast))])
    outs = pl.pallas_call(
        functools.partial(_ffn_kernel, mode=mode, n_prompt_blocks=npb, n_steps=n_steps,
                          cast_idx=tuple(job[1] for job in cast_jobs)),
        grid=(n_steps,),
        in_specs=x_specs + [_stacked(gains, g_idx), _stacked(gains, g_idx + 1)]
        + [_stacked(w, k) for w, k in weights] + [any_spec] * n_cast,
        out_specs=out_specs,
        out_shape=out_shape,
        scratch_shapes=scratch,
        input_output_aliases=aliases,
        compiler_params=pltpu.CompilerParams(
            dimension_semantics=("arbitrary",), vmem_limit_bytes=vmem_limit),
        name="ffn_" + mode,
    )(*xs, gains, gains, *[w for w, _ in weights], *[job[0] for job in cast_jobs])
    return outs[:n_main_out], list(outs[n_main_out:])


def _conv_body(x_ref, o_ref, newhist_ref, ext_ref, gpre_ref, gpost_ref, win_ref, ck_ref,
               wout_ref, *, rows, stride, pad, first_step, start_pos):
    del first_step, start_pos
    x = x_ref[...]
    h = _rmsnorm(x, gpre_ref[...]).astype(_BF16)
    ck = ck_ref[...]
    us = []
    for c in range(0, D_MODEL, FF_TILE):
        cols = slice(c, c + FF_TILE)
        gate_c = _dot(h, win_ref[:, D_MODEL + c:D_MODEL + c + FF_TILE])
        v = _dot(h, win_ref[:, 2 * D_MODEL + c:2 * D_MODEL + c + FF_TILE])
        z = gate_c * v
        ext_ref[pl.ds(pad, rows), cols] = z
        conv = (ck[0:1, cols] * ext_ref[pl.ds(pad - 2 * stride, rows), cols]
                + ck[1:2, cols] * ext_ref[pl.ds(pad - stride, rows), cols]
                + ck[2:3, cols] * z)
        gate_b = _dot(h, win_ref[:, cols])
        us.append((gate_b * conv).astype(_BF16))
    y = _dot(jnp.concatenate(us, axis=-1), wout_ref[...])
    o_ref[...] = x + _rmsnorm(y, gpost_ref[...])

    tail = ext_ref[pl.ds(rows, pad), :]
    newhist_ref[...] = tail.reshape(newhist_ref.shape)
    ext_ref[pl.ds(0, pad), :] = tail


def _pool_body(x_ref, o_ref, newhist_ref, ext_ref, gpre_ref, gpost_ref, wgrp_ref, scale_ref,
               *, rows, stride, pad, first_step, start_pos):
    x = x_ref[...]
    u = _rmsnorm(x, gpre_ref[...])
    ext_ref[pl.ds(pad, rows), :] = u

    step = lax.broadcasted_iota(jnp.int32, (rows, 1), 0) // stride
    pos = step + first_step + start_pos

    ys = []
    for g, w in enumerate(POOL_WINDOWS):
        cols = slice(g * POOL_GROUP, (g + 1) * POOL_GROUP)
        acc = ext_ref[:, cols]
        span = 1
        while span < w:
            acc = acc + pltpu.roll(acc, span * stride, axis=0)
            span *= 2
        win_sum = acc[pad:, :]
        count = jnp.minimum(pos + 1, w).astype(_F32)
        diff = win_sum / count - u[:, cols]
        ys.append(_dot(diff, wgrp_ref[g]))
    y = jnp.concatenate(ys, axis=-1) * scale_ref[...]
    o_ref[...] = x + _rmsnorm(y, gpost_ref[...])

    tail = ext_ref[pl.ds(rows, pad), :]
    newhist_ref[...] = tail.reshape(newhist_ref.shape)
    ext_ref[pl.ds(0, pad), :] = tail


def _mixer_kernel(x_ref, hist_ref, *refs, body, n_params, rows, blocks_per_seq,
                  n_prompt_blocks, pad_p, pad_s, stride_s):
    params = refs[:n_params]
    o_ref, newhist_p_ref, newhist_s_ref, ext_p_ref, ext_s_ref = refs[n_params:]
    i = pl.program_id(0)

    @pl.when(i < n_prompt_blocks)
    def _():
        j = lax.rem(i, jnp.int32(blocks_per_seq))

        @pl.when(j == 0)
        def _():
            ext_p_ref[pl.ds(0, pad_p), :] = jnp.zeros((pad_p, D_MODEL), _F32)

        body(x_ref, o_ref, newhist_p_ref, ext_p_ref, *params, rows=rows, stride=1, pad=pad_p,
             first_step=j * rows, start_pos=0)

    @pl.when(i == n_prompt_blocks)
    def _():
        ext_s_ref[pl.ds(0, pad_s), :] = hist_ref[...]
        body(x_ref, o_ref, newhist_s_ref, ext_s_ref, *params, rows=rows, stride=stride_s,
             pad=pad_s, first_step=0, start_pos=PAST_LEN)


def _mixer_sublayer(body, name, x_all, hist_s, h_idx, params, *, n_seq, seq_len, n_sample,
                    stride_s, pad_p, pad_s):
    rows = MIX_ROWS
    assert seq_len % rows == 0 and n_sample == rows and hist_s.shape[1] == pad_s
    blocks_per_seq = seq_len // rows
    npb = n_seq * blocks_per_seq
    row_spec = pl.BlockSpec((rows, D_MODEL), lambda i: (i, 0))
    kernel = functools.partial(
        _mixer_kernel, body=body, n_params=len(params), rows=rows, blocks_per_seq=blocks_per_seq,
        n_prompt_blocks=npb, pad_p=pad_p, pad_s=pad_s, stride_s=stride_s)
    return pl.pallas_call(
        kernel,
        grid=(npb + 1,),
        in_specs=[row_spec, _stacked(hist_s, h_idx)] + [_stacked(a, k) for a, k in params],
        out_specs=[row_spec,
                   pl.BlockSpec((1, pad_p, D_MODEL),
                                lambda i: (jnp.minimum(i, npb - 1) // blocks_per_seq, 0, 0)),
                   pl.BlockSpec((pad_s, D_MODEL), lambda i: (0, 0))],
        out_shape=[jax.ShapeDtypeStruct(x_all.shape, _F32),
                   jax.ShapeDtypeStruct((n_seq, pad_p, D_MODEL), _F32),
                   jax.ShapeDtypeStruct((pad_s, D_MODEL), _F32)],
        scratch_shapes=[pltpu.VMEM((pad_p + rows, D_MODEL), _F32),
                        pltpu.VMEM((pad_s + rows, D_MODEL), _F32)],
        input_output_aliases={0: 0},
        compiler_params=pltpu.CompilerParams(
            dimension_semantics=("arbitrary",), vmem_limit_bytes=MIXER_VMEM_LIMIT_BYTES),
        name=name,
    )(x_all, hist_s, *[a for a, _ in params])


def kernel(x_prompt, x_sample, state_conv, state_pool, norm_gains, ffn_w_gate, ffn_w_up,
           ffn_w_down, conv_w_in, conv_kernel, conv_w_out, pool_w_group, pool_scale):
    batch, seq, _ = x_prompt.shape
    dec_batch, dec_seq, _ = x_sample.shape
    n_prompt = batch * seq
    n_sample = dec_batch * dec_seq
    n_conv = conv_w_in.shape[0]
    n_pool = pool_w_group.shape[0]

    xp_rows = x_prompt.reshape(n_prompt, D_MODEL)
    xs_rows = jnp.swapaxes(x_sample, 0, 1).reshape(n_sample, D_MODEL)

    gains = norm_gains.reshape(DEPTH * N_NORMS, 1, D_MODEL)
    ffn_w = (ffn_w_gate.reshape(2 * DEPTH, D_MODEL, D_FF),
             ffn_w_up.reshape(2 * DEPTH, D_MODEL, D_FF),
             ffn_w_down.reshape(2 * DEPTH, D_FF, D_MODEL))
    scale = pool_scale.reshape(n_pool, 1, D_MODEL)

    conv_pad_s = (CONV_W - 1) * dec_batch
    pool_pad_s = POOL_HIST * dec_batch
    conv_hist_s = jnp.swapaxes(state_conv, 1, 2).reshape(n_conv, conv_pad_s, D_MODEL)
    pool_hist_s = jnp.swapaxes(state_pool, 1, 2).reshape(n_pool, pool_pad_s, D_MODEL)
    ffn = functools.partial(_ffn_sublayer, n_prompt=n_prompt, n_sample=n_sample)
    weights = [(w, 0) for w in ffn_w]

    def ffn_cast_jobs(idx):
        return [(w, idx) for w in ffn_w]
    mixer = functools.partial(_mixer_sublayer, n_seq=batch, seq_len=seq, n_sample=n_sample,
                              stride_s=dec_batch)

    new_conv_p, new_conv_s, new_pool_p, new_pool_s = [], [], [], []
    for i in range(DEPTH):
        g0 = i * N_NORMS
        xs = (xp_rows, xs_rows) if i == 0 else (x_all,)
        j = i // 2
        jobs = ffn_cast_jobs(2 * i + 1)
        if i % 2 == 0:
            jobs += [(conv_w_in, j), (conv_w_out, j)]
        (x_all,), cast = ffn(xs, gains, g0, weights, jobs, mode="first" if i == 0 else "mid")
        weights, conv16 = [(w, None) for w in cast[:3]], cast[3:]
        norms = [(gains, g0 + 2), (gains, g0 + 3)]
        if i % 2 == 0:
            x_all, hp, hs = mixer(
                _conv_body, "conv_mixer", x_all, conv_hist_s, j,
                norms + [(conv16[0], None), (conv_kernel, j), (conv16[1], None)],
                pad_p=CONV_PAD_ROWS, pad_s=conv_pad_s)
            new_conv_p.append(hp[:, CONV_PAD_ROWS - (CONV_W - 1):, :])
            new_conv_s.append(jnp.swapaxes(hs.reshape(CONV_W - 1, dec_batch, D_MODEL), 0, 1))
        else:
            x_all, hp, hs = mixer(
                _pool_body, "pool_mixer", x_all, pool_hist_s, j,
                norms + [(pool_w_group, j), (scale, j)],
                pad_p=POOL_PAD_ROWS, pad_s=pool_pad_s)
            new_pool_p.append(hp[:, POOL_PAD_ROWS - POOL_HIST:, :])
            new_pool_s.append(jnp.swapaxes(hs.reshape(POOL_HIST, dec_batch, D_MODEL), 0, 1))
        if i == DEPTH - 1:
            (yp_rows, ys_rows), _ = ffn((x_all,), gains, g0 + 4, weights, [], mode="last")
        else:
            (x_all,), cast = ffn((x_all,), gains, g0 + 4, weights, ffn_cast_jobs(2 * i + 2),
                                 mode="mid")
            weights = [(w, None) for w in cast]

    y_prompt = yp_rows.reshape(batch, seq, D_MODEL)
    y_sample = jnp.swapaxes(ys_rows.reshape(dec_seq, dec_batch, D_MODEL), 0, 1)
    return (y_prompt, y_sample, jnp.stack(new_conv_p), jnp.stack(new_conv_s),
            jnp.stack(new_pool_p), jnp.stack(new_pool_s))
```
